```python
import math
import jax, jax.numpy as jnp
from jax import lax
import numpy as np

D_MODEL = 2048
BATCH = 4
SEQ = 4096
DEPTH = 4

MIX_WIDTH = D_MODEL // 2
N_BRANCH = 4
Q_BLOCK = 128
DA_QK_DIM = 64
DA_V_DIM = 2 * DA_QK_DIM
DA_HEADS = MIX_WIDTH // DA_V_DIM
HG_DK = 128
HG_DV = 128
HG_HEADS = MIX_WIDTH // HG_DK
HG_CHUNK = 64
RW_HEAD = 64
RW_HEADS = MIX_WIDTH // RW_HEAD
RW_W_RANK = 64
RW_A_RANK = 64
RW_V_RANK = 32
RW_LN_EPS = 64e-5
SB_DIM = 128
SB_HEADS = MIX_WIDTH // SB_DIM
REL_BUCKETS = 32
REL_MAX_DIST = 128
LN_EPS = 1e-5
RMS_EPS = 1e-6
DEEPNORM_ALPHA = (2 * DEPTH) ** 0.25
DEEPNORM_BETA = (8 * DEPTH) ** -0.25

RW_MIX = 3 * MIX_WIDTH + RW_W_RANK + RW_A_RANK
COL_SIZES = (
    DA_HEADS * 2 * DA_QK_DIM, DA_HEADS * 2 * DA_QK_DIM, DA_HEADS * DA_V_DIM, MIX_WIDTH,
    HG_HEADS * HG_DK, HG_HEADS * HG_DK, HG_HEADS * HG_DV, MIX_WIDTH,
    RW_MIX, MIX_WIDTH,
    SB_HEADS * SB_DIM, SB_HEADS * SB_DIM, SB_HEADS * SB_DIM, MIX_WIDTH,
    N_BRANCH * D_MODEL,
)
IN_COLS = sum(COL_SIZES)
SPLIT_POINTS = tuple(int(c) for c in np.cumsum(COL_SIZES)[:-1])
RW_SPLITS = (MIX_WIDTH, 2 * MIX_WIDTH, 3 * MIX_WIDTH, 3 * MIX_WIDTH + RW_W_RANK)
MERGE_SPLITS = tuple(D_MODEL * i for i in range(1, N_BRANCH))

kernel_name = 'hybrid_gated_branch_deepnorm_trunk'


def layer_norm(x, g, b, eps=LN_EPS):
    xf = x.astype(jnp.float32)
    mu = jnp.mean(xf, -1, keepdims=True)
    var = jnp.mean(jnp.square(xf - mu), -1, keepdims=True)
    return ((xf - mu) * lax.rsqrt(var + eps) * g + b).astype(x.dtype)


def rms_norm(x, g, eps=RMS_EPS):
    xf = x.astype(jnp.float32)
    return (xf * lax.rsqrt(jnp.mean(xf * xf, -1, keepdims=True) + eps) * g).astype(x.dtype)


def token_shift(z):
    return jnp.pad(z[:, :-1], ((0, 0), (1, 0), (0, 0)))


def t5_bucket(dist):
    max_exact = REL_BUCKETS // 2
    n = jnp.maximum(dist, 0)
    nf = jnp.maximum(n, 1).astype(jnp.float32)
    large = max_exact + (jnp.log(nf / max_exact) / math.log(REL_MAX_DIST / max_exact)
                         * (REL_BUCKETS - max_exact)).astype(jnp.int32)
    large = jnp.minimum(large, REL_BUCKETS - 1)
    return jnp.where(n < max_exact, n, large)


def diff_attention(q, k, v, rel_table, lam, subln_g, layer):
    B, S, H, _, Dk = q.shape
    Dv = v.shape[-1]
    lam_init = 0.8 - 0.6 * math.exp(-0.3 * layer)
    lamf = lam.astype(jnp.float32)
    lam_full = jnp.exp(jnp.sum(lamf[0] * lamf[1])) - jnp.exp(jnp.sum(lamf[2] * lamf[3])) + lam_init
    nb = S // Q_BLOCK
    qb = q.reshape(B, nb, Q_BLOCK, H, 2, Dk).transpose(1, 0, 2, 3, 4, 5)
    kpos = jnp.arange(S)
    scale = Dk ** -0.5

    def block(args):
        qi, bi = args
        qpos = bi * Q_BLOCK + jnp.arange(Q_BLOCK)
        dist = qpos[:, None] - kpos[None, :]
        bias = rel_table[t5_bucket(dist)].astype(jnp.float32).transpose(2, 0, 1)
        s = jnp.einsum('bqhmd,bkhmd->bhmqk', qi, k).astype(jnp.float32) * scale + bias[None, :, None]
        s = jnp.where((dist >= 0)[None, None, None], s, -jnp.inf)
        p = jax.nn.softmax(s, axis=-1)
        a = p[:, :, 0] - lam_full * p[:, :, 1]
        return jnp.einsum('bhqk,bkhd->bqhd', a.astype(v.dtype), v)

    o = lax.map(block, (qb, jnp.arange(nb)))
    o = o.transpose(1, 0, 2, 3, 4).reshape(B, S, H, Dv)
    o = rms_norm(o, subln_g) * (1.0 - lam_init)
    return o.reshape(B, S, H * Dv)


def hgrn2_chunked(q, log_f, k, i):
    B, S, H, Dk = q.shape
    Dv = i.shape[-1]
    nc = S // HG_CHUNK
    f32 = jnp.float32

    def to_chunks(t):
        return t.astype(f32).reshape(B, nc, HG_CHUNK, H, t.shape[-1]).transpose(1, 0, 3, 2, 4)

    causal = jnp.tril(jnp.ones((HG_CHUNK, HG_CHUNK), dtype=bool))[:, :, None]

    def step(state, xs):
        qc, lfc, kc, ic = xs
        b = jnp.cumsum(lfc, axis=2)
        inter = jnp.einsum('bhcd,bhde->bhce', qc * jnp.exp(b), state)
        diff = b[:, :, :, None, :] - b[:, :, None, :, :]
        decay = jnp.exp(jnp.where(causal, diff, -jnp.inf))
        scores = jnp.sum(qc[:, :, :, None, :] * decay * kc[:, :, None, :, :], axis=-1)
        intra = jnp.einsum('bhts,bhse->bhte', scores, ic)
        b_last = b[:, :, -1:, :]
        state = (jnp.exp(b_last[:, :, 0, :])[..., None] * state
                 + jnp.einsum('bhsd,bhse->bhde', kc * jnp.exp(b_last - b), ic))
        return state, inter + intra

    state0 = jnp.zeros((B, H, Dk, Dv), f32)
    _, o = lax.scan(step, state0, (to_chunks(q), to_chunks(log_f), to_chunks(k), to_chunks(i)))
    return o.transpose(1, 0, 3, 2, 4).reshape(B, S, H, Dv)


def rwkv7_scan(r, w, k, v, a, b):
    B, S, H, N = r.shape

    def step(state, xs):
        rt, wt, kt, vt, at, bt = xs
        sa = jnp.einsum('bhvk,bhk->bhv', state, at)
        state = state * wt[:, :, None, :] + sa[..., None] * bt[:, :, None, :] + vt[..., None] * kt[:, :, None, :]
        return state, jnp.einsum('bhvk,bhk->bhv', state, rt)

    xs = tuple(t.transpose(1, 0, 2, 3) for t in (r, w, k, v, a, b))
    _, o = lax.scan(step, jnp.zeros((B, H, N, N), jnp.float32), xs)
    return o.transpose(1, 0, 2, 3)


def rwkv7_branch(zm, zg, v_gate_logit, v_first, mu, w0, w2, a0, a2, k_k, k_a, r_k, lnx_g, lnx_b):
    B, S, _ = zm.shape
    dt = zm.dtype
    f32 = jnp.float32
    zm = (zm + (token_shift(zm) - zm) * mu).astype(f32)
    r, k, v, wd, ad = jnp.split(zm, RW_SPLITS, axis=-1)
    w_log = -jax.nn.softplus(-(w0 + jnp.tanh(wd) @ w2)) - 0.5
    decay = jnp.exp(-jnp.exp(w_log))
    a = jax.nn.sigmoid(a0 + ad @ a2)
    if v_first is None:
        v_first = v
    else:
        v = v + (v_first - v) * jax.nn.sigmoid(v_gate_logit.astype(f32))

    def hd(t):
        return t.reshape(B, S, RW_HEADS, RW_HEAD)

    kk = hd(k * k_k)
    kk = kk / jnp.maximum(jnp.sqrt(jnp.sum(kk * kk, -1, keepdims=True)), 1e-12)
    k = k * (1.0 + (a - 1.0) * k_a)
    rh, kh, vh, ah = hd(r), hd(k), hd(v), hd(a)
    o = rwkv7_scan(rh, hd(decay), kh, vh, -kk, kk * ah)
    mu_o = jnp.mean(o, -1, keepdims=True)
    var_o = jnp.mean(jnp.square(o - mu_o), -1, keepdims=True)
    g_h = lnx_g.reshape(RW_HEADS, RW_HEAD)
    b_h = lnx_b.reshape(RW_HEADS, RW_HEAD)
    o = (o - mu_o) * lax.rsqrt(var_o + RW_LN_EPS) * g_h + b_h
    o = o + jnp.sum(rh * kh * r_k, -1, keepdims=True) * vh
    y = o.reshape(B, S, MIX_WIDTH).astype(dt) * jax.nn.silu(zg)
    return y, v_first


def stick_breaking_attention(q, k, v):
    B, S, H, D = q.shape
    nb = S // Q_BLOCK
    qb = q.reshape(B, nb, Q_BLOCK, H, D).transpose(1, 0, 2, 3, 4)
    kpos = jnp.arange(S)
    scale = D ** -0.5

    def block(args):
        qi, bi = args
        qpos = bi * Q_BLOCK + jnp.arange(Q_BLOCK)
        strict = (qpos[:, None] > kpos[None, :])[None, None]
        z = jnp.einsum('bqhd,bkhd->bhqk', qi, k).astype(jnp.float32) * scale
        log_beta = jax.nn.log_sigmoid(z)
        log_keep = jnp.where(strict, jax.nn.log_sigmoid(-z), 0.0)
        between = lax.cumsum(log_keep, axis=3, reverse=True) - log_keep
        wgt = jnp.where(strict, jnp.exp(log_beta + between), 0.0)
        return jnp.einsum('bhqk,bkhd->bqhd', wgt.astype(v.dtype), v)

    o = lax.map(block, (qb, jnp.arange(nb)))
    return o.transpose(1, 0, 2, 3, 4).reshape(B, S, H * D)


def setup_inputs(seed: int = 0) -> dict:
    key = jax.random.key(seed)
    ks = jax.random.split(key, 32)
    nrm = jax.random.normal
    f32 = jnp.float32
    L1 = DEPTH - 1
    return {
        'x': nrm(ks[0], (BATCH, SEQ, D_MODEL), f32),
        'w_in': nrm(ks[1], (DEPTH, D_MODEL, IN_COLS), f32) * D_MODEL ** -0.5,
        'rel_bias': nrm(ks[2], (REL_BUCKETS, DA_HEADS), f32) * 0.5,
        'da_lambda': nrm(ks[3], (DEPTH, 4, DA_QK_DIM), f32) * 0.1,
        'da_subln': 1.0 + 0.02 * nrm(ks[4], (DEPTH, DA_V_DIM), f32),
        'hg_lower': nrm(ks[5], (DEPTH, HG_HEADS * HG_DK), f32) * 0.5,
        'hg_norm': 1.0 + 0.02 * nrm(ks[6], (DEPTH, HG_DV), f32),
        'rw_mu': jax.random.uniform(ks[7], (DEPTH, RW_MIX), f32),
        'rw_w0': jax.random.uniform(ks[8], (DEPTH, MIX_WIDTH), f32, -6.0, -1.0),
        'rw_w2': nrm(ks[9], (DEPTH, RW_W_RANK, MIX_WIDTH), f32) * 0.1 * RW_W_RANK ** -0.5,
        'rw_a0': nrm(ks[10], (DEPTH, MIX_WIDTH), f32) * 0.1,
        'rw_a2': nrm(ks[11], (DEPTH, RW_A_RANK, MIX_WIDTH), f32) * 0.5 * RW_A_RANK ** -0.5,
        'rw_v1': nrm(ks[12], (L1, D_MODEL, RW_V_RANK), f32) * D_MODEL ** -0.5,
        'rw_v_mu': jax.random.uniform(ks[13], (L1, RW_V_RANK), f32),
        'rw_v0': 1.0 + 0.1 * nrm(ks[14], (L1, MIX_WIDTH), f32),
        'rw_v2': nrm(ks[15], (L1, RW_V_RANK, MIX_WIDTH), f32) * 0.5 * RW_V_RANK ** -0.5,
        'rw_kk': 0.85 + 0.05 * nrm(ks[16], (DEPTH, MIX_WIDTH), f32),
        'rw_ka': 1.0 + 0.05 * nrm(ks[17], (DEPTH, MIX_WIDTH), f32),
        'rw_rk': nrm(ks[18], (DEPTH, RW_HEADS, RW_HEAD), f32) * 0.1,
        'rw_lnx_g': 1.0 + 0.02 * nrm(ks[19], (DEPTH, MIX_WIDTH), f32),
        'rw_lnx_b': 0.02 * nrm(ks[20], (DEPTH, MIX_WIDTH), f32),
        'w_branch': nrm(ks[21], (DEPTH, N_BRANCH, MIX_WIDTH, D_MODEL), f32) * MIX_WIDTH ** -0.5,
        'w_out': nrm(ks[22], (DEPTH, D_MODEL, D_MODEL), f32) * D_MODEL ** -0.5 * DEEPNORM_BETA,
        'ln_g': 1.0 + 0.02 * nrm(ks[23], (DEPTH, D_MODEL), f32),
        'ln_b': 0.02 * nrm(ks[24], (DEPTH, D_MODEL), f32),
    }


def reference(x, w_in, rel_bias, da_lambda, da_subln, hg_lower, hg_norm, rw_mu, rw_w0, rw_w2, rw_a0,
              rw_a2, rw_v1, rw_v_mu, rw_v0, rw_v2, rw_kk, rw_ka, rw_rk, rw_lnx_g, rw_lnx_b,
              w_branch, w_out, ln_g, ln_b):
    B, S, _ = x.shape
    f32 = jnp.float32
    lbs = jnp.cumsum(jax.nn.softmax(hg_lower.astype(f32), axis=0), axis=0)
    lbs = lbs - lbs[0:1]
    h = x
    v_first = None
    for l in range(DEPTH):
        z = h @ w_in[l]
        (a_q, a_k, a_v, a_g, h_q, h_f, h_i, h_g, r_mix, r_g,
         s_q, s_k, s_v, s_g, m_g) = jnp.split(z, SPLIT_POINTS, axis=-1)

        y_a = diff_attention(a_q.reshape(B, S, DA_HEADS, 2, DA_QK_DIM), a_k.reshape(B, S, DA_HEADS, 2, DA_QK_DIM),
                             a_v.reshape(B, S, DA_HEADS, DA_V_DIM), rel_bias, da_lambda[l], da_subln[l], l)
        y_a = y_a * jax.nn.silu(a_g)

        lb = lbs[l]
        zf = h_f.astype(f32)
        log_f = jnp.logaddexp(jnp.log(lb), jnp.log1p(-lb) + jax.nn.log_sigmoid(zf))
        k_in = (1.0 - lb) * jax.nn.sigmoid(-zf)
        o_h = hgrn2_chunked(h_q.reshape(B, S, HG_HEADS, HG_DK), log_f.reshape(B, S, HG_HEADS, HG_DK),
                            k_in.reshape(B, S, HG_HEADS, HG_DK), h_i.reshape(B, S, HG_HEADS, HG_DV))
        y_b = rms_norm(o_h, hg_norm[l]).reshape(B, S, MIX_WIDTH).astype(h.dtype) * jax.nn.silu(h_g)

        if l == 0:
            v_gate_logit = None
        else:
            vd = h @ rw_v1[l - 1]
            vd = vd + (token_shift(vd) - vd) * rw_v_mu[l - 1]
            v_gate_logit = rw_v0[l - 1] + vd @ rw_v2[l - 1]
        y_c, v_first = rwkv7_branch(r_mix, r_g, v_gate_logit, v_first, rw_mu[l], rw_w0[l], rw_w2[l], rw_a0[l],
                                    rw_a2[l], rw_kk[l], rw_ka[l], rw_rk[l], rw_lnx_g[l], rw_lnx_b[l])

        y_d = stick_breaking_attention(s_q.reshape(B, S, SB_HEADS, SB_DIM), s_k.reshape(B, S, SB_HEADS, SB_DIM),
                                       s_v.reshape(B, S, SB_HEADS, SB_DIM)) * jax.nn.silu(s_g)

        gates = jnp.split(m_g, MERGE_SPLITS, axis=-1)
        branches = (y_a, y_b, y_c, y_d)
        merged = jax.nn.sigmoid(gates[0]) * (branches[0] @ w_branch[l, 0])
        for n in range(1, N_BRANCH):
            merged = merged + jax.nn.sigmoid(gates[n]) * (branches[n] @ w_branch[l, n])
        out = merged @ w_out[l]
        h = layer_norm(DEEPNORM_ALPHA * h + out, ln_g[l], ln_b[l])
    return h
```

```python
import functools
import math

import numpy as np
import jax
import jax.numpy as jnp
from jax import lax
from jax.experimental import pallas as pl
from jax.experimental.pallas import tpu as pltpu

F32 = jnp.float32
BF16 = jnp.bfloat16

V7X_LANES = 128
V7X_VMEM_LIMIT_BYTES = 56 * 1024 * 1024

N_BRANCH = 4
DA_QK = 64
REL_BUCKETS = 32
REL_MAX_DIST = 128
HG_CHUNK = 64
RW_HEAD = 64
RW_W_RANK = 64
RW_A_RANK = 64
RW_V_RANK = 32
RW_LN_EPS = 64e-5
RW_CHUNK = 64
LN_EPS = 1e-5
RMS_EPS = 1e-6
NEG_BIG = -1e30

ATT_TILE = 256
RW_BLOCK = 512


def _params(*sem):
    return pltpu.CompilerParams(dimension_semantics=sem, vmem_limit_bytes=V7X_VMEM_LIMIT_BYTES)


def _dot(a, b):
    return jnp.dot(a, b, preferred_element_type=F32)


def _dot_nt(a, b):
    return lax.dot_general(a, b, (((1,), (1,)), ((), ())), preferred_element_type=F32)


def _dot_tn(a, b):
    return lax.dot_general(a, b, (((0,), (0,)), ((), ())), preferred_element_type=F32)


def _sigmoid(x):
    return 1.0 / (1.0 + jnp.exp(-x))


def _silu(x):
    return x * _sigmoid(x)


def _log_sigmoid(x):
    return jnp.minimum(x, 0.0) - jnp.log1p(jnp.exp(-jnp.abs(x)))


def _split_bf16(x):
    hi = x.astype(BF16)
    lo = (x - hi.astype(F32)).astype(BF16)
    return hi, lo


def _matmul_kernel(x_ref, w_ref, o_ref):
    o_ref[...] = _dot(x_ref[...], w_ref[...]).astype(o_ref.dtype)


def _matmul(x, w, out_dtype, tm, tn, name):
    m, k = x.shape
    n = w.shape[1]
    return pl.pallas_call(
        _matmul_kernel,
        grid=(m // tm, n // tn),
        in_specs=[pl.BlockSpec((tm, k), lambda i, j: (i, 0)),
                  pl.BlockSpec((k, tn), lambda i, j: (0, j))],
        out_specs=pl.BlockSpec((tm, tn), lambda i, j: (i, j)),
        out_shape=jax.ShapeDtypeStruct((m, n), out_dtype),
        compiler_params=_params("parallel", "arbitrary"),
        name=name,
    )(x, w)


def _t5_bucket_np(dist):
    max_exact = REL_BUCKETS // 2
    n = np.maximum(dist, 0)
    nf = np.maximum(n, 1).astype(np.float32)
    large = max_exact + (np.log(nf / np.float32(max_exact)) / np.float32(math.log(REL_MAX_DIST / max_exact))
                         * np.float32(REL_BUCKETS - max_exact)).astype(np.int32)
    large = np.minimum(large, REL_BUCKETS - 1)
    return np.where(n < max_exact, n, large)


def _bias_tiles(rel_bias, t):
    assert t + 1 >= REL_MAX_DIST
    rows = np.arange(t)[:, None]
    cols = np.arange(t)[None, :]
    tiles = []
    for behind in range(3):
        dist = behind * t + rows - cols
        tile = rel_bias[_t5_bucket_np(dist)].astype(F32)
        tile = jnp.where((dist >= 0)[:, :, None], tile, NEG_BIG)
        tiles.append(tile)
    return jnp.stack(tiles, 0).transpose(3, 0, 1, 2)


def _da_kernel(lam_ref, q_ref, k_ref, v_ref, g_ref, bias_ref, subln_ref, o_ref, *, lam_init, t):
    i = pl.program_id(2)
    lane = lax.broadcasted_iota(jnp.int32, (t, 2 * DA_QK), 1)
    q = q_ref[0] * (DA_QK ** -0.5)
    zero = jnp.zeros_like(q)
    q_maps = (jnp.where(lane < DA_QK, q, zero), jnp.where(lane >= DA_QK, q, zero))

    def body(jj, carry):
        j = i - jj
        kj = k_ref[0, pl.ds(pl.multiple_of(j * t, t), t), :]
        vj = v_ref[0, pl.ds(pl.multiple_of(j * t, t), t), :]
        bias = bias_ref[0, jnp.minimum(jj, 2)]
        out = []
        for m_i, (m_run, l_run, acc) in enumerate(carry):
            s = _dot_nt(q_maps[m_i], kj) + bias
            m_new = jnp.maximum(m_run, jnp.max(s, axis=-1, keepdims=True))
            alpha = jnp.exp(m_run - m_new)
            p = jnp.exp(s - m_new)
            l_new = alpha * l_run + jnp.sum(p, axis=-1, keepdims=True)
            acc_new = alpha * acc + _dot(p.astype(BF16), vj)
            out.append((m_new, l_new, acc_new))
        return tuple(out)

    init = tuple((jnp.full((t, 1), NEG_BIG, F32), jnp.zeros((t, 1), F32), jnp.zeros((t, 2 * DA_QK), F32))
                 for _ in range(2))
    (m0, l0, acc0), (m1, l1, acc1) = lax.fori_loop(0, i + 1, body, init)

    lam = lam_ref[...]
    lam_full = (jnp.exp(jnp.sum(lam[0:1] * lam[1:2], axis=-1, keepdims=True))
                - jnp.exp(jnp.sum(lam[2:3] * lam[3:4], axis=-1, keepdims=True)) + lam_init)
    o = acc0 / l0 - lam_full * (acc1 / l1)
    o = o * lax.rsqrt(jnp.mean(o * o, axis=-1, keepdims=True) + RMS_EPS) * subln_ref[...] * (1.0 - lam_init)
    o_ref[0] = (o * _silu(g_ref[0].astype(F32))).astype(o_ref.dtype)


def _diff_attention(z, cols, rel_bias, lam, subln, layer, n_heads):
    b, s, _ = z.shape
    t = min(ATT_TILE, s)
    lam_init = 0.8 - 0.6 * math.exp(-0.3 * layer)
    cq, ck, cv, cg = cols
    w = 2 * DA_QK
    return pl.pallas_call(
        functools.partial(_da_kernel, lam_init=lam_init, t=t),
        grid=(b, n_heads, s // t),
        in_specs=[
            pl.BlockSpec((4, DA_QK), lambda bi, h, i: (0, 0)),
            pl.BlockSpec((1, t, w), lambda bi, h, i: (bi, i, cq + h)),
            pl.BlockSpec((1, s, w), lambda bi, h, i: (bi, 0, ck + h)),
            pl.BlockSpec((1, s, w), lambda bi, h, i: (bi, 0, cv + h)),
            pl.BlockSpec((1, t, w), lambda bi, h, i: (bi, i, cg + h)),
            pl.BlockSpec((1, 3, t, t), lambda bi, h, i: (h, 0, 0, 0)),
            pl.BlockSpec((1, w), lambda bi, h, i: (0, 0)),
        ],
        out_specs=pl.BlockSpec((1, t, w), lambda bi, h, i: (bi, i, h)),
        out_shape=jax.ShapeDtypeStruct((b, s, n_heads * w), BF16),
        compiler_params=_params("parallel", "parallel", "arbitrary"),
        name="diff_attention",
    )(lam.astype(F32), z, z, z, z, _bias_tiles(rel_bias, t), subln.astype(F32).reshape(1, w))


def _sb_kernel(q_ref, k_ref, v_ref, g_ref, tri_ref, o_ref, *, t, d):
    i = pl.program_id(2)
    q = q_ref[0]
    scale = d ** -0.5
    tri = tri_ref[...]
    rows = lax.broadcasted_iota(jnp.int32, (t, t), 0)
    cols = lax.broadcasted_iota(jnp.int32, (t, t), 1)
    strict = rows > cols

    def block(j, later, acc, diagonal):
        kj = k_ref[0, pl.ds(pl.multiple_of(j * t, t), t), :]
        vj = v_ref[0, pl.ds(pl.multiple_of(j * t, t), t), :]
        zs = _dot_nt(q, kj) * scale
        log_beta = _log_sigmoid(zs)
        log_keep = log_beta - zs
        if diagonal:
            log_keep = jnp.where(strict, log_keep, 0.0)
        hi, lo = _split_bf16(log_keep)
        between = later + _dot(hi, tri) + _dot(lo, tri)
        wgt = jnp.exp(log_beta + between)
        if diagonal:
            wgt = jnp.where(strict, wgt, 0.0)
        acc = acc + _dot(wgt.astype(BF16), vj)
        later = later + jnp.sum(log_keep, axis=-1, keepdims=True)
        return later, acc

    later, acc = block(i, jnp.zeros((t, 1), F32), jnp.zeros((t, d), F32), True)

    def body(jj, carry):
        return block(i - 1 - jj, carry[0], carry[1], False)

    later, acc = lax.fori_loop(0, i, body, (later, acc))
    o_ref[0] = (acc * _silu(g_ref[0].astype(F32))).astype(o_ref.dtype)


def _stick_breaking(z, cols, n_heads, d):
    b, s, _ = z.shape
    t = min(ATT_TILE, s)
    cq, ck, cv, cg = cols
    tri = jnp.asarray(np.tril(np.ones((t, t), np.float32), -1), BF16)
    return pl.pallas_call(
        functools.partial(_sb_kernel, t=t, d=d),
        grid=(b, n_heads, s // t),
        in_specs=[
            pl.BlockSpec((1, t, d), lambda bi, h, i: (bi, i, cq + h)),
            pl.BlockSpec((1, s, d), lambda bi, h, i: (bi, 0, ck + h)),
            pl.BlockSpec((1, s, d), lambda bi, h, i: (bi, 0, cv + h)),
            pl.BlockSpec((1, t, d), lambda bi, h, i: (bi, i, cg + h)),
            pl.BlockSpec((t, t), lambda bi, h, i: (0, 0)),
        ],
        out_specs=pl.BlockSpec((1, t, d), lambda bi, h, i: (bi, i, h)),
        out_shape=jax.ShapeDtypeStruct((b, s, n_heads * d), BF16),
        compiler_params=_params("parallel", "parallel", "arbitrary"),
        name="stick_breaking",
    )(z, z, z, z, tri)


def _hg_level_sizes(c):
    sizes = []
    m = c // 2
    while m >= 1:
        sizes.append(m)
        m //= 2
    return sizes


def _hg_sum_matrix(c):
    r = np.arange(c)[:, None]
    s = np.arange(c)[None, :]
    blocks = []
    for m in _hg_level_sizes(c):
        start = (r // m) * m
        upper = (r // m) % 2 == 1
        lower_part = (s > r) & (s <= start + m - 1)
        blocks.append(np.where(upper, (s >= start) & (s <= r), lower_part))
    blocks.append(s <= r)
    blocks.append(s > r)
    return np.concatenate(blocks, 0).astype(np.float32)


def _hg_level_masks(c):
    r = np.arange(c)[:, None]
    s = np.arange(c)[None, :]
    masks = [((r // m) % 2 == 1) & ((s // m) == (r // m) - 1) for m in _hg_level_sizes(c)]
    masks.append(r == s)
    return np.stack(masks, 0).astype(np.float32)


def _hg_kernel(q_ref, f_ref, i_ref, g_ref, lb_ref, nrm_ref, sums_ref, masks_ref, o_ref, *, c, n_chunks):
    n_levels = len(_hg_level_sizes(c))
    lb = lb_ref[...]
    log_lb = jnp.log(lb)
    log_1m_lb = jnp.log1p(-lb)
    sums = sums_ref[...]
    nrm = nrm_ref[...]

    def body(ci, state_t):
        rows = pl.ds(pl.multiple_of(ci * c, c), c)
        q = q_ref[0, rows, :].astype(F32)
        zf = f_ref[0, rows, :].astype(F32)
        inp = i_ref[0, rows, :]
        e = jnp.exp(-jnp.abs(zf))
        log_sig = jnp.minimum(zf, 0.0) - jnp.log1p(e)
        arg = log_1m_lb + log_sig
        log_f = jnp.maximum(log_lb, arg) + jnp.log1p(jnp.exp(-jnp.abs(log_lb - arg)))
        k_in = (1.0 - lb) * jnp.where(zf >= 0.0, e, 1.0) / (1.0 + e)
        hi, lo = _split_bf16(log_f)
        d_all = jnp.exp(_dot(sums, hi) + _dot(sums, lo))
        scores = masks_ref[n_levels] * _dot_nt(q.astype(BF16), k_in.astype(BF16))
        for lv in range(n_levels):
            x = d_all[lv * c:(lv + 1) * c]
            scores = scores + masks_ref[lv] * _dot_nt((q * x).astype(BF16), (k_in * x).astype(BF16))
        eb = d_all[n_levels * c:(n_levels + 1) * c]
        rev = d_all[(n_levels + 1) * c:(n_levels + 2) * c]
        o = _dot_nt((q * eb).astype(BF16), state_t.astype(BF16)) + _dot(scores.astype(BF16), inp)
        state_t = state_t * eb[c - 1:c] + _dot_tn(inp, (k_in * rev).astype(BF16))
        y = o * lax.rsqrt(jnp.mean(o * o, axis=-1, keepdims=True) + RMS_EPS) * nrm
        o_ref[0, rows, :] = (y * _silu(g_ref[0, rows, :].astype(F32))).astype(o_ref.dtype)
        return state_t

    dk = q_ref.shape[-1]
    lax.fori_loop(0, n_chunks, body, jnp.zeros((i_ref.shape[-1], dk), F32))


def _hgrn2(z, cols, lb, nrm, n_heads):
    b, s, _ = z.shape
    c = HG_CHUNK
    d = V7X_LANES
    cq, cf, ci, cg = cols
    sums = jnp.asarray(_hg_sum_matrix(c), BF16)
    masks = jnp.asarray(_hg_level_masks(c), F32)
    seq = lambda off: pl.BlockSpec((1, s, d), lambda bi, h: (bi, 0, off + h))
    return pl.pallas_call(
        functools.partial(_hg_kernel, c=c, n_chunks=s // c),
        grid=(b, n_heads),
        in_specs=[seq(cq), seq(cf), seq(ci), seq(cg),
                  pl.BlockSpec((1, d), lambda bi, h: (0, h)),
                  pl.BlockSpec((1, d), lambda bi, h: (0, 0)),
                  pl.BlockSpec(sums.shape, lambda bi, h: (0, 0)),
                  pl.BlockSpec(masks.shape, lambda bi, h: (0, 0, 0))],
        out_specs=pl.BlockSpec((1, s, d), lambda bi, h: (bi, 0, h)),
        out_shape=jax.ShapeDtypeStruct((b, s, n_heads * d), BF16),
        compiler_params=_params("parallel", "parallel"),
        name="hgrn2",
    )(z, z, z, z, lb.astype(F32).reshape(1, n_heads * d), nrm.astype(F32).reshape(1, d), sums, masks)


RW_LOW = 256


def _rw_prep_kernel(*refs, first):
    if first:
        (zm_ref, zm_prev_ref, low_ref, low_prev_ref, mu_ref, mu_low_ref, lora_ref, w0_ref, a0_ref, v0_ref,
         kk_ref, ka_ref, r_out, lw_out, k_out, v_out, a_out, kk_out) = refs
        vf_ref = None
    else:
        (zm_ref, zm_prev_ref, low_ref, low_prev_ref, vf_ref, mu_ref, mu_low_ref, lora_ref, w0_ref, a0_ref,
         v0_ref, kk_ref, ka_ref, r_out, lw_out, k_out, v_out, a_out, kk_out) = refs
    i = pl.program_id(1)
    width = r_out.shape[-1]

    def lerp_shift(x, prev_block, mu):
        row = lax.broadcasted_iota(jnp.int32, x.shape, 0)
        prev = jnp.where(i > 0, prev_block[7:8, :], 0.0)
        shifted = jnp.where(row == 0, prev, pltpu.roll(x, 1, 0))
        return x + (shifted - x) * mu

    zm = lerp_shift(zm_ref[0].astype(F32), zm_prev_ref[0].astype(F32), mu_ref[...])
    low = lerp_shift(low_ref[0], low_prev_ref[0], mu_low_ref[...])
    lane = lax.broadcasted_iota(jnp.int32, low.shape, 1)
    low = jnp.where(lane < RW_W_RANK, jnp.tanh(low), low)
    lora = _dot(low.astype(BF16), lora_ref[...])
    r = zm[:, 0:width]
    k = zm[:, width:2 * width]
    v = zm[:, 2 * width:3 * width]
    x = -(w0_ref[...] + lora[:, 0:width])
    w_log = -(jnp.maximum(x, 0.0) + jnp.log1p(jnp.exp(-jnp.abs(x)))) - 0.5
    a = _sigmoid(a0_ref[...] + lora[:, width:2 * width])
    if not first:
        v = v + (vf_ref[0] - v) * _sigmoid(v0_ref[...] + lora[:, 2 * width:3 * width])
    r_out[0] = r
    lw_out[0] = -jnp.exp(w_log)
    k_out[0] = k * (1.0 + (a - 1.0) * ka_ref[...])
    v_out[0] = v
    a_out[0] = a
    kk_out[0] = k * kk_ref[...]


def _rw_chunk_kernel(r_ref, lw_ref, k_ref, v_ref, a_ref, kk_ref, g_ref, rk_ref, lng_ref, lnb_ref, tri_ref,
                     o_ref, state_ref, *, c, n_chunks):
    hd = RW_HEAD

    @pl.when(pl.program_id(2) == 0)
    def _():
        state_ref[...] = jnp.zeros_like(state_ref)

    lane = lax.broadcasted_iota(jnp.int32, (1, 2 * hd), 1)
    head_masks = ((lane < hd).astype(F32), (lane >= hd).astype(F32))
    in_h0 = lane < hd
    tri_inc = tri_ref[0]
    tri_strict = tri_ref[1]
    tri_inc_b = tri_inc.astype(BF16)
    eye_c = tri_inc - tri_strict
    rr = lax.broadcasted_iota(jnp.int32, (2 * hd, 2 * hd), 0)
    cc = lax.broadcasted_iota(jnp.int32, (2 * hd, 2 * hd), 1)
    eye_p = rr == cc

    def head_sum(x):
        s0 = jnp.sum(x * head_masks[0], axis=-1, keepdims=True)
        s1 = jnp.sum(x * head_masks[1], axis=-1, keepdims=True)
        return jnp.where(in_h0, s0, s1)

    def body(ci, carry):
        rows = pl.ds(pl.multiple_of(ci * c, c), c)
        r = r_ref[0, rows, :]
        lw = lw_ref[0, rows, :]
        k = k_ref[0, rows, :]
        v = v_ref[0, rows, :]
        a = a_ref[0, rows, :]
        kk = kk_ref[0, rows, :]
        hi, lo = _split_bf16(lw)
        g = _dot(tri_inc_b, hi) + _dot(tri_inc_b, lo)
        g_last = g[c - 1:c]
        e_in = jnp.exp(g)
        e_ex = jnp.exp(g - lw)
        e_neg = jnp.exp(-g)
        e_rev = jnp.exp(g_last - g)
        kkn = kk / jnp.maximum(jnp.sqrt(head_sum(kk * kk)), 1e-12)
        b_vec = kkn * a
        a_t = -kkn * e_ex
        r_t = r * e_in
        b_hat = (b_vec * e_neg).astype(BF16)
        k_hat = (k * e_neg).astype(BF16)
        b_til = b_vec * e_rev
        k_til = k * e_rev
        g_pair = jnp.zeros((c, 2 * hd), F32)
        h_pair = jnp.zeros((c, 2 * hd), F32)
        p_pair = jnp.where(eye_p, jnp.exp(g_last), 0.0)
        q_pair = jnp.zeros((2 * hd, 2 * hd), F32)
        for mh in head_masks:
            a_h = (a_t * mh).astype(BF16)
            x = jnp.concatenate([a_h, (r_t * mh).astype(BF16)], axis=0)
            sb = _dot_nt(x, b_hat)
            sk = _dot_nt(x, k_hat)
            l_ab = sb[:c] * tri_strict
            m_rb = (sb[c:] * tri_inc).astype(BF16)
            l_ak = (sk[:c] * tri_strict).astype(BF16)
            m_rk = (sk[c:] * tri_inc).astype(BF16)
            inv = eye_c + l_ab
            power = l_ab
            steps = int(math.log2(c)) - 1
            for _ in range(steps):
                pb = power.astype(BF16)
                power = _dot(pb, pb)
                inv = inv + _dot(inv.astype(BF16), power.astype(BF16))
            inv_b = inv.astype(BF16)
            v_h = (v * mh).astype(BF16)
            ta = _dot(inv_b, a_h).astype(BF16)
            w1 = _dot(inv_b, _dot(l_ak, v_h).astype(BF16)).astype(BF16)
            b_h = (b_til * mh).astype(BF16)
            g_pair = g_pair + r_t * mh + _dot(m_rb, ta)
            h_pair = h_pair + _dot(m_rb, w1) + _dot(m_rk, v_h)
            p_pair = p_pair + _dot_tn(b_h, ta)
            q_pair = q_pair + _dot_tn(b_h, w1) + _dot_tn((k_til * mh).astype(BF16), v_h)
        state = state_ref[...]
        s_hi, s_lo = _split_bf16(state)
        gp = jnp.concatenate([g_pair, p_pair], axis=0).astype(BF16)
        res = _dot(gp, s_hi) + _dot(gp, s_lo)
        o = res[:c] + h_pair
        state_ref[...] = res[c:] + q_pair

        mean = head_sum(o) * (1.0 / hd)
        cen = o - mean
        var = head_sum(cen * cen) * (1.0 / hd)
        o = cen * lax.rsqrt(var + RW_LN_EPS) * lng_ref[...] + lnb_ref[...]
        o = o + head_sum(r * k * rk_ref[...]) * v
        o_ref[0, rows, :] = (o * _silu(g_ref[0, rows, :].astype(F32))).astype(o_ref.dtype)
        return carry

    lax.fori_loop(0, n_chunks, body, 0)


def _rwkv7(zm, zg, low, v_first, mu, w0, w2, a0, a2, v_mu, v0, v2, k_k, k_a, r_k, lnx_g, lnx_b,
           zm_col=0, zg_col=0):
    b, s, _ = zm.shape
    width = w0.shape[0]
    first = v_first is None
    tm = min(256, s)
    row = lambda x: x.astype(F32).reshape(1, -1)
    mu_low = jnp.concatenate([mu[3 * width:], v_mu if not first else jnp.zeros((RW_V_RANK,), F32),
                              jnp.zeros((RW_LOW - RW_W_RANK - RW_A_RANK - RW_V_RANK,), F32)])
    lora = jnp.zeros((RW_LOW, 3 * width), F32)
    lora = lora.at[0:RW_W_RANK, 0:width].set(w2)
    lora = lora.at[RW_W_RANK:RW_W_RANK + RW_A_RANK, width:2 * width].set(a2)
    if not first:
        lora = lora.at[RW_W_RANK + RW_A_RANK:RW_W_RANK + RW_A_RANK + RW_V_RANK, 2 * width:].set(v2)
        v0_row = row(v0)
    else:
        v0_row = jnp.zeros((1, width), F32)
    blk3 = 3 * width // V7X_LANES
    tile = lambda w_: pl.BlockSpec((1, tm, w_), lambda bi, i: (bi, i, 0))
    prev = lambda w_, col: pl.BlockSpec((1, 8, w_), lambda bi, i: (bi, jnp.maximum(i * (tm // 8) - 1, 0), col))
    full = lambda x: pl.BlockSpec(x.shape, lambda bi, i: (0,) * x.ndim)
    mu_row, mu_low_row, lora_b = row(mu[:3 * width]), row(mu_low), lora.astype(BF16)
    consts = [mu_row, mu_low_row, lora_b, row(w0), row(a0), v0_row, row(k_k), row(k_a)]
    assert zm_col % blk3 == 0
    in_specs = [pl.BlockSpec((1, tm, 3 * width), lambda bi, i: (bi, i, zm_col // blk3)),
                prev(3 * width, zm_col // blk3), tile(RW_LOW), prev(RW_LOW, 0)]
    args = [zm, zm, low, low]
    if not first:
        in_specs.append(tile(width))
        args.append(v_first)
    in_specs += [full(x) for x in consts]
    args += consts
    outs = pl.pallas_call(
        functools.partial(_rw_prep_kernel, first=first),
        grid=(b, s // tm),
        in_specs=in_specs,
        out_specs=[tile(width)] * 6,
        out_shape=[jax.ShapeDtypeStruct((b, s, width), F32)] * 6,
        compiler_params=_params("parallel", "parallel"),
        name="rwkv_prep",
    )(*args)
    r, lw, k, v, a, kk = outs

    c = RW_CHUNK
    blk = min(RW_BLOCK, s)
    pw = 2 * RW_HEAD
    n_pairs = width // pw
    tri = jnp.asarray(np.stack([np.tril(np.ones((c, c), np.float32)), np.tril(np.ones((c, c), np.float32), -1)]))
    seq = lambda off=0: pl.BlockSpec((1, blk, pw), lambda bi, p, i: (bi, i, off + p))
    par = pl.BlockSpec((1, pw), lambda bi, p, i: (0, p))
    y = pl.pallas_call(
        functools.partial(_rw_chunk_kernel, c=c, n_chunks=blk // c),
        grid=(b, n_pairs, s // blk),
        in_specs=[seq()] * 6 + [seq(zg_col), par, par, par, pl.BlockSpec(tri.shape, lambda bi, p, i: (0, 0, 0))],
        out_specs=seq(),
        out_shape=jax.ShapeDtypeStruct((b, s, width), BF16),
        scratch_shapes=[pltpu.VMEM((pw, pw), F32)],
        compiler_params=_params("parallel", "parallel", "arbitrary"),
        name="rwkv_chunk",
    )(r, lw, k, v, a, kk, zg, row(r_k), row(lnx_g), row(lnx_b), tri)
    return y, (v if first else v_first)


def _merge_kernel(ya_ref, yb_ref, yc_ref, yd_ref, g0_ref, g1_ref, g2_ref, g3_ref, wb_ref, o_ref):
    acc = None
    for n, (y_ref, g_ref) in enumerate(zip((ya_ref, yb_ref, yc_ref, yd_ref), (g0_ref, g1_ref, g2_ref, g3_ref))):
        term = _sigmoid(g_ref[...].astype(F32)) * _dot(y_ref[...], wb_ref[n])
        acc = term if acc is None else acc + term
    o_ref[...] = acc.astype(o_ref.dtype)


def _merge(ys, z2d, gate_col, wb):
    m, w = ys[0].shape
    d = wb.shape[-1]
    tm = min(256, m)
    y_spec = pl.BlockSpec((tm, w), lambda i: (i, 0))
    g_spec = lambda n: pl.BlockSpec((tm, d), lambda i: (i, gate_col + n))
    return pl.pallas_call(
        _merge_kernel,
        grid=(m // tm,),
        in_specs=[y_spec] * N_BRANCH + [g_spec(n) for n in range(N_BRANCH)]
        + [pl.BlockSpec(wb.shape, lambda i: (0, 0, 0))],
        out_specs=pl.BlockSpec((tm, d), lambda i: (i, 0)),
        out_shape=jax.ShapeDtypeStruct((m, d), BF16),
        compiler_params=_params("parallel"),
        name="merge",
    )(*ys, z2d, z2d, z2d, z2d, wb)


def _out_ln_kernel(m_ref, w_ref, h_ref, g_ref, b_ref, o_ref, ob_ref, *, alpha):
    x = alpha * h_ref[...] + _dot(m_ref[...], w_ref[...])
    mu = jnp.mean(x, axis=-1, keepdims=True)
    cen = x - mu
    var = jnp.mean(cen * cen, axis=-1, keepdims=True)
    y = cen * lax.rsqrt(var + LN_EPS) * g_ref[...] + b_ref[...]
    o_ref[...] = y
    ob_ref[...] = y.astype(ob_ref.dtype)


def _out_ln(merged, w_out, h, ln_g, ln_b, alpha):
    m, d = h.shape
    tm = min(512, m)
    tile = pl.BlockSpec((tm, d), lambda i: (i, 0))
    vec = pl.BlockSpec((1, d), lambda i: (0, 0))
    return pl.pallas_call(
        functools.partial(_out_ln_kernel, alpha=alpha),
        grid=(m // tm,),
        in_specs=[tile, pl.BlockSpec(w_out.shape, lambda i: (0, 0)), tile, vec, vec],
        out_specs=[tile, tile],
        out_shape=[jax.ShapeDtypeStruct((m, d), F32), jax.ShapeDtypeStruct((m, d), BF16)],
        compiler_params=_params("parallel"),
        name="out_ln",
    )(merged, w_out, h, ln_g.astype(F32).reshape(1, d), ln_b.astype(F32).reshape(1, d))


def kernel(x, w_in, rel_bias, da_lambda, da_subln, hg_lower, hg_norm, rw_mu, rw_w0, rw_w2, rw_a0, rw_a2, rw_v1,
           rw_v_mu, rw_v0, rw_v2, rw_kk, rw_ka, rw_rk, rw_lnx_g, rw_lnx_b, w_branch, w_out, ln_g, ln_b):
    b, s, d = x.shape
    depth = w_in.shape[0]
    width = w_branch.shape[2]
    heads = width // V7X_LANES
    m = b * s
    alpha = (2 * depth) ** 0.25
    gb = width // V7X_LANES

    off_a, off_b, off_c = 0, 4 * width, 8 * width
    rw_mix = 3 * width + RW_W_RANK + RW_A_RANK
    off_cg = off_c + rw_mix
    off_d = off_cg + width
    off_m = off_d + 4 * width
    col_c, col_cg, col_a, col_b, col_d = 0, 3 * gb, 4 * gb, 8 * gb, 12 * gb
    col_m = 16 * gb

    lbs = jnp.cumsum(jax.nn.softmax(hg_lower.astype(F32), axis=0), axis=0)
    lbs = lbs - lbs[0:1]

    h = x.reshape(m, d)
    hb = h.astype(BF16)
    v_first = None
    for l in range(depth):
        wl = w_in[l]
        w_main = jnp.concatenate([wl[:, off_c:off_c + 3 * width], wl[:, off_cg:off_cg + width],
                                  wl[:, off_a:off_a + 4 * width], wl[:, off_b:off_b + 4 * width],
                                  wl[:, off_d:off_d + 4 * width], wl[:, off_m:]], axis=1).astype(BF16)
        low_cols = [wl[:, off_c + 3 * width:off_cg]]
        if l > 0:
            low_cols.append(rw_v1[l - 1])
        n_low = sum(c.shape[1] for c in low_cols)
        low_cols.append(jnp.zeros((d, RW_LOW - n_low), F32))
        w_low = jnp.concatenate(low_cols, axis=1).astype(BF16)

        z2d = _matmul(hb, w_main, BF16, min(1024, m), 1024, "in_proj")
        low = _matmul(hb, w_low, F32, min(1024, m), RW_LOW, "in_proj_low").reshape(b, s, RW_LOW)
        z = z2d.reshape(b, s, -1)

        y_a = _diff_attention(z, (col_a, col_a + gb, col_a + 2 * gb, col_a + 3 * gb), rel_bias, da_lambda[l],
                              da_subln[l], l, heads)
        y_b = _hgrn2(z, (col_b, col_b + gb, col_b + 2 * gb, col_b + 3 * gb), lbs[l], hg_norm[l], heads)
        y_c, v_first = _rwkv7(z, z, low, v_first, rw_mu[l], rw_w0[l], rw_w2[l], rw_a0[l], rw_a2[l],
                              rw_v_mu[l - 1] if l > 0 else None, rw_v0[l - 1] if l > 0 else None,
                              rw_v2[l - 1] if l > 0 else None, rw_kk[l], rw_ka[l], rw_rk[l], rw_lnx_g[l],
                              rw_lnx_b[l], zm_col=col_c, zg_col=col_cg)
        y_d = _stick_breaking(z, (col_d, col_d + gb, col_d + 2 * gb, col_d + 3 * gb), heads, V7X_LANES)

        merged = _merge([y.reshape(m, width) for y in (y_a, y_b, y_c, y_d)], z2d, col_m * V7X_LANES // d,
                        w_branch[l].astype(BF16))
        h, hb = _out_ln(merged, w_out[l].astype(BF16), h, ln_g[l], ln_b[l], alpha)
    return h.reshape(b, s, d)
```

```python
import functools
import math

import numpy as np
import jax
import jax.numpy as jnp
from jax import lax
from jax.experimental import pallas as pl
from jax.experimental.pallas import tpu as pltpu

F32 = jnp.float32
BF16 = jnp.bfloat16

V7X_LANES = 128
V7X_VMEM_LIMIT_BYTES = 56 * 1024 * 1024

N_BRANCH = 4
DA_QK = 64
REL_BUCKETS = 32
REL_MAX_DIST = 128
HG_CHUNK = 64
RW_HEAD = 64
RW_W_RANK = 64
RW_A_RANK = 64
RW_V_RANK = 32
RW_LN_EPS = 64e-5
RW_CHUNK = 64
LN_EPS = 1e-5
RMS_EPS = 1e-6
NEG_BIG = -1e30

ATT_TILE = 512
ATT_STRIP = 64
HG_GROUP = 8
RW_BLOCK = 512
RW_PAIRS = 4


def _params(*sem):
    return pltpu.CompilerParams(dimension_semantics=sem, vmem_limit_bytes=V7X_VMEM_LIMIT_BYTES)


def _dot(a, b):
    return jnp.dot(a, b, preferred_element_type=F32)


def _dot_nt(a, b):
    return lax.dot_general(a, b, (((1,), (1,)), ((), ())), preferred_element_type=F32)


def _dot_tn(a, b):
    return lax.dot_general(a, b, (((0,), (0,)), ((), ())), preferred_element_type=F32)


def _sigmoid(x):
    return 1.0 / (1.0 + jnp.exp(-x))


def _silu(x):
    return x * _sigmoid(x)


def _split_bf16(x):
    hi = x.astype(BF16)
    lo = (x - hi.astype(F32)).astype(BF16)
    return hi, lo


def _lock_step(generators):
    live = list(generators)
    while live:
        for gen in list(live):
            try:
                next(gen)
            except StopIteration:
                live.remove(gen)


def _matmul_kernel(x_ref, w_ref, o_ref):
    o_ref[...] = _dot(x_ref[...], w_ref[...]).astype(o_ref.dtype)


def _matmul(x, w, out_dtype, tm, tn, name):
    m, k = x.shape
    n = w.shape[1]
    return pl.pallas_call(
        _matmul_kernel,
        grid=(m // tm, n // tn),
        in_specs=[pl.BlockSpec((tm, k), lambda i, j: (i, 0)),
                  pl.BlockSpec((k, tn), lambda i, j: (0, j))],
        out_specs=pl.BlockSpec((tm, tn), lambda i, j: (i, j)),
        out_shape=jax.ShapeDtypeStruct((m, n), out_dtype),
        compiler_params=_params("parallel", "arbitrary"),
        name=name,
    )(x, w)


def _t5_bucket_np(dist):
    max_exact = REL_BUCKETS // 2
    n = np.maximum(dist, 0)
    nf = np.maximum(n, 1).astype(np.float32)
    large = max_exact + (np.log(nf / np.float32(max_exact)) / np.float32(math.log(REL_MAX_DIST / max_exact))
                         * np.float32(REL_BUCKETS - max_exact)).astype(np.int32)
    large = np.minimum(large, REL_BUCKETS - 1)
    return np.where(n < max_exact, n, large)


def _bias_tiles(rel_bias, t):
    assert t + 1 >= REL_MAX_DIST
    rows = np.arange(t)[:, None]
    cols = np.arange(t)[None, :]
    tiles = []
    for behind in range(3):
        dist = behind * t + rows - cols
        tile = rel_bias[_t5_bucket_np(dist)].astype(F32)
        tile = jnp.where((dist >= 0)[:, :, None], tile, NEG_BIG)
        tiles.append(tile)
    tiles.append(jnp.full_like(tiles[0], NEG_BIG))
    return jnp.stack(tiles, 0).transpose(3, 0, 1, 2)


def _causal_walk(nq, depth, n_kinds):
    steps = [(i, i - jj, min(jj, n_kinds - 2), int(jj == 0), int(jj == i)) for i in range(nq) for jj in range(i + 1)]
    pad = (steps[-1][0], steps[-1][1], n_kinds - 1, 0, 0)
    return np.asarray(steps + [pad] * (depth - 1), np.int32).T


def _da_kernel(tab_ref, lam_ref, q_ref, k_ref, v_ref, g_ref, bias_ref, subln_ref, o_ref,
               qm_ref, s_ref, p_ref, alpha_ref, m_ref, acc_ref, *, lam_init, t, n_steps, strip):
    w = 2 * DA_QK
    lane = lax.broadcasted_iota(jnp.int32, q_ref.shape[1:], 1)
    q = q_ref[0] * (DA_QK ** -0.5)
    zero = jnp.zeros_like(q)
    qm_ref[0] = jnp.where(lane < DA_QK, q, zero)
    qm_ref[1] = jnp.where(lane >= DA_QK, q, zero)
    s_ref[...] = jnp.zeros(s_ref.shape, F32)
    p_ref[...] = jnp.zeros(p_ref.shape, BF16)
    alpha_ref[...] = jnp.ones(alpha_ref.shape, F32)
    m_ref[...] = jnp.zeros(m_ref.shape, F32)
    acc_ref[...] = jnp.zeros(acc_ref.shape, F32)
    ones = jnp.ones((t, w), BF16)
    lam = lam_ref[...]
    lam_full = (jnp.exp(jnp.sum(lam[0:1] * lam[1:2], axis=-1, keepdims=True))
                - jnp.exp(jnp.sum(lam[2:3] * lam[3:4], axis=-1, keepdims=True)) + lam_init)
    dead = bias_ref.shape[1] - 1

    def tile_rows(idx):
        return pl.ds(pl.multiple_of(idx * t, t), t)

    def body(it, carry):
        c = jnp.maximum(it - 2, 0)
        v_ext = jnp.concatenate([v_ref[0, tile_rows(tab_ref[1, c]), :], ones], axis=1)
        for m_i in range(2):
            acc_ref[m_i] = jnp.tile(alpha_ref[m_i], (1, 2)) * acc_ref[m_i] + _dot(p_ref[m_i], v_ext)

        b = jnp.maximum(it - 1, 0)
        kind = jnp.where(it >= 1, tab_ref[2, b], dead)
        first = jnp.logical_and(it >= 1, tab_ref[3, b] == 1)
        for r0 in range(0, t, strip):
            rows = pl.ds(r0, strip)
            bias = bias_ref[0, kind, rows, :]
            for m_i in range(2):
                s = s_ref[m_i, rows, :] + bias
                m_prev = jnp.where(first, NEG_BIG, m_ref[m_i, rows, :])
                m_new = jnp.maximum(m_prev, jnp.max(s, axis=-1, keepdims=True))
                p_ref[m_i, rows, :] = jnp.exp(s - jnp.tile(m_new, (1, t // w))).astype(BF16)
                alpha_ref[m_i, rows, :] = jnp.exp(m_prev - m_new)
                m_ref[m_i, rows, :] = m_new

        kj = k_ref[0, tile_rows(tab_ref[1, it]), :]
        q_rows = tile_rows(tab_ref[0, it])
        for m_i in range(2):
            s_ref[m_i] = _dot_nt(qm_ref[m_i, q_rows, :], kj)

        @pl.when(jnp.logical_and(it >= 2, tab_ref[4, c] == 1))
        def _():
            rows = tile_rows(tab_ref[0, c])
            o = (acc_ref[0, :, 0:w] / acc_ref[0, :, w:2 * w]
                 - lam_full * (acc_ref[1, :, 0:w] / acc_ref[1, :, w:2 * w]))
            o = o * lax.rsqrt(jnp.mean(o * o, axis=-1, keepdims=True) + RMS_EPS) * subln_ref[...] * (1.0 - lam_init)
            o_ref[0, rows, :] = (o * _silu(g_ref[0, rows, :].astype(F32))).astype(o_ref.dtype)

        return carry

    lax.fori_loop(0, n_steps + 2, body, 0)


def _diff_attention(z, cols, rel_bias, lam, subln, layer, n_heads):
    b, s, _ = z.shape
    t = min(ATT_TILE, s)
    lam_init = 0.8 - 0.6 * math.exp(-0.3 * layer)
    cq, ck, cv, cg = cols
    w = 2 * DA_QK
    tab = _causal_walk(s // t, 3, 4)
    n_steps = tab.shape[1] - 2
    seq = lambda off: pl.BlockSpec((1, s, w), lambda bi, h, tab_: (bi, 0, off + h))
    return pl.pallas_call(
        functools.partial(_da_kernel, lam_init=lam_init, t=t, n_steps=n_steps, strip=min(ATT_STRIP, t)),
        grid_spec=pltpu.PrefetchScalarGridSpec(
            num_scalar_prefetch=1,
            grid=(b, n_heads),
            in_specs=[
                pl.BlockSpec((4, DA_QK), lambda bi, h, tab_: (0, 0)),
                seq(cq), seq(ck), seq(cv), seq(cg),
                pl.BlockSpec((1, 4, t, t), lambda bi, h, tab_: (h, 0, 0, 0)),
                pl.BlockSpec((1, w), lambda bi, h, tab_: (0, 0)),
            ],
            out_specs=pl.BlockSpec((1, s, w), lambda bi, h, tab_: (bi, 0, h)),
            scratch_shapes=[pltpu.VMEM((2, s, w), BF16), pltpu.VMEM((2, t, t), F32), pltpu.VMEM((2, t, t), BF16),
                            pltpu.VMEM((2, t, w), F32), pltpu.VMEM((2, t, w), F32),
                            pltpu.VMEM((2, t, 2 * w), F32)],
        ),
        out_shape=jax.ShapeDtypeStruct((b, s, n_heads * w), BF16),
        compiler_params=_params("parallel", "parallel"),
        name="diff_attention",
    )(jnp.asarray(tab), lam.astype(F32), z, z, z, z, _bias_tiles(rel_bias, t), subln.astype(F32).reshape(1, w))


SB_SUB = 256


def _sb_kernel(tab_ref, q_ref, k_ref, v_ref, g_ref, tri_ref, neg_ref, o_ref,
               qs_ref, s_ref, lb_ref, lk_ref, lt_ref, later_ref, p_ref, acc_ref, *, t, d, n_steps):
    n_sub = t // SB_SUB
    qs_ref[...] = (q_ref[0].astype(F32) * (d ** -0.5 * math.log2(math.e))).astype(BF16)
    s_ref[...] = jnp.zeros(s_ref.shape, F32)
    lb_ref[...] = jnp.full(lb_ref.shape, NEG_BIG, F32)
    lk_ref[...] = jnp.zeros(lk_ref.shape, BF16)
    lt_ref[...] = jnp.zeros(lt_ref.shape, F32)
    later_ref[...] = jnp.zeros(later_ref.shape, F32)
    p_ref[...] = jnp.zeros(p_ref.shape, BF16)
    acc_ref[...] = jnp.zeros(acc_ref.shape, F32)
    tri = tri_ref[...]
    dead = neg_ref.shape[0] - 1

    def tile_rows(idx):
        return pl.ds(pl.multiple_of(idx * t, t), t)

    def body(it, carry):
        c4 = jnp.maximum(it - 3, 0)
        acc_ref[...] += _dot(p_ref[...], v_ref[0, tile_rows(tab_ref[1, c4]), :])

        for c in range(n_sub):
            cols = slice(c * SB_SUB, (c + 1) * SB_SUB)
            between = jnp.tile(lt_ref[c], (1, SB_SUB // d)) + _dot(lk_ref[:, cols], tri)
            p_ref[:, cols] = jnp.exp2(lb_ref[:, cols] + between).astype(BF16)

        b2 = jnp.maximum(it - 1, 0)
        kind = jnp.where(it >= 1, tab_ref[2, b2], dead)
        carry_on = jnp.where(jnp.logical_and(it >= 1, tab_ref[3, b2] == 1), 0.0, 1.0)
        zs = s_ref[...] + neg_ref[kind]
        log_beta = jnp.minimum(zs, 0.0) - jnp.log2(1.0 + jnp.exp2(jnp.minimum(zs, -zs)))
        log_keep = log_beta - zs
        lb_ref[...] = log_beta
        lk_ref[...] = log_keep.astype(BF16)
        later = later_ref[...] * carry_on
        for c in reversed(range(n_sub)):
            lt_ref[c] = later
            later = later + jnp.sum(log_keep[:, c * SB_SUB:(c + 1) * SB_SUB], axis=-1, keepdims=True)
        later_ref[...] = later

        s_ref[...] = _dot_nt(qs_ref[tile_rows(tab_ref[0, it]), :], k_ref[0, tile_rows(tab_ref[1, it]), :])

        @pl.when(jnp.logical_and(it >= 3, tab_ref[4, c4] == 1))
        def _():
            rows = tile_rows(tab_ref[0, c4])
            o_ref[0, rows, :] = (acc_ref[...] * _silu(g_ref[0, rows, :].astype(F32))).astype(o_ref.dtype)
            acc_ref[...] = jnp.zeros(acc_ref.shape, F32)

        return carry

    lax.fori_loop(0, n_steps + 3, body, 0)


def _stick_breaking(z, cols, n_heads, d):
    b, s, _ = z.shape
    t = min(ATT_TILE, s)
    cq, ck, cv, cg = cols
    tab = _causal_walk(s // t, 4, 3)
    n_steps = tab.shape[1] - 3
    tri = jnp.asarray(np.tril(np.ones((SB_SUB, SB_SUB), np.float32), -1), BF16)
    strict = np.tril(np.ones((t, t), np.float32), -1)
    neg = jnp.asarray(np.stack([(strict - 1.0), np.zeros_like(strict), -np.ones_like(strict)]) * -NEG_BIG)
    seq = lambda off: pl.BlockSpec((1, s, d), lambda bi, h, tab_: (bi, 0, off + h))
    const = lambda x: pl.BlockSpec(x.shape, lambda bi, h, tab_: (0,) * x.ndim)
    return pl.pallas_call(
        functools.partial(_sb_kernel, t=t, d=d, n_steps=n_steps),
        grid_spec=pltpu.PrefetchScalarGridSpec(
            num_scalar_prefetch=1,
            grid=(b, n_heads),
            in_specs=[seq(cq), seq(ck), seq(cv), seq(cg), const(tri), const(neg)],
            out_specs=pl.BlockSpec((1, s, d), lambda bi, h, tab_: (bi, 0, h)),
            scratch_shapes=[pltpu.VMEM((s, d), BF16), pltpu.VMEM((t, t), F32), pltpu.VMEM((t, t), F32),
                            pltpu.VMEM((t, t), BF16), pltpu.VMEM((t // SB_SUB, t, d), F32),
                            pltpu.VMEM((t, d), F32), pltpu.VMEM((t, t), BF16), pltpu.VMEM((t, d), F32)],
        ),
        out_shape=jax.ShapeDtypeStruct((b, s, n_heads * d), BF16),
        compiler_params=_params("parallel", "parallel"),
        name="stick_breaking",
    )(jnp.asarray(tab), z, z, z, z, tri, neg)


def _hg_level_sizes(c):
    sizes = []
    m = c // 2
    while m >= 1:
        sizes.append(m)
        m //= 2
    return sizes


def _hg_sum_matrix(c):
    r = np.arange(c)[:, None]
    s = np.arange(c)[None, :]
    blocks = []
    for m in _hg_level_sizes(c):
        start = (r // m) * m
        upper = (r // m) % 2 == 1
        lower_part = (s > r) & (s <= start + m - 1)
        blocks.append(np.where(upper, (s >= start) & (s <= r), lower_part))
    blocks.append(s <= r)
    blocks.append(s > r)
    return np.concatenate(blocks, 0).astype(np.float32)


def _hg_level_masks(c):
    r = np.arange(c)[:, None]
    s = np.arange(c)[None, :]
    masks = [((r // m) % 2 == 1) & ((s // m) == (r // m) - 1) for m in _hg_level_sizes(c)]
    masks.append(r == s)
    return np.stack(masks, 0).astype(np.float32)


def _hg_kernel(q_ref, f_ref, i_ref, g_ref, lb_ref, nrm_ref, sums_ref, masks_ref, o_ref, *, c, n_chunks, group):
    n_levels = len(_hg_level_sizes(c))
    lb = lb_ref[...]
    log_lb = jnp.log(lb)
    log_1m_lb = jnp.log1p(-lb)
    sums = sums_ref[...]
    nrm = nrm_ref[...]
    dk = q_ref.shape[-1]

    def chunk_pair(first, slot, out):
        rows = pl.ds(pl.multiple_of(first * c, 2 * c), 2 * c)
        q2 = q_ref[0, rows, :].astype(F32)
        zf = f_ref[0, rows, :].astype(F32)
        inp2 = i_ref[0, rows, :]
        e = jnp.exp(-jnp.abs(zf))
        log_sig = jnp.minimum(zf, 0.0) - jnp.log1p(e)
        arg = log_1m_lb + log_sig
        log_f = jnp.maximum(log_lb, arg) + jnp.log1p(jnp.exp(-jnp.abs(log_lb - arg)))
        k2 = (1.0 - lb) * jnp.where(zf >= 0.0, e, 1.0) / (1.0 + e)
        lf = jnp.concatenate([log_f[:c], log_f[c:]], axis=1).astype(BF16)
        d_cat = jnp.exp(_dot(sums, lf))
        yield
        halves = []
        for u in range(2):
            q = q2[u * c:(u + 1) * c]
            k_in = k2[u * c:(u + 1) * c]
            d_all = d_cat[:, u * dk:(u + 1) * dk]
            prods = [_dot_nt(q.astype(BF16), k_in.astype(BF16))]
            for lv in range(n_levels):
                x = d_all[lv * c:(lv + 1) * c]
                prods.append(_dot_nt((q * x).astype(BF16), (k_in * x).astype(BF16)))
            halves.append((q, k_in, d_all, prods))
        yield
        for u, (q, k_in, d_all, prods) in enumerate(halves):
            scores = masks_ref[n_levels] * prods[0]
            for lv in range(n_levels):
                scores = scores + masks_ref[lv] * prods[lv + 1]
            eb = d_all[n_levels * c:(n_levels + 1) * c]
            rev = d_all[(n_levels + 1) * c:(n_levels + 2) * c]
            inp = inp2[u * c:(u + 1) * c]
            out[slot + u] = ((q * eb).astype(BF16), _dot(scores.astype(BF16), inp), eb[c - 1:c],
                             _dot_tn(inp, (k_in * rev).astype(BF16)))
        yield

    def body(gi, state_t):
        base = gi * group
        pre = {}
        _lock_step([chunk_pair(base + u, u, pre) for u in range(0, group, 2)])
        states = []
        for u in range(group):
            _, _, decay, update = pre[u]
            states.append(state_t.astype(BF16))
            state_t = state_t * decay + update
        inter = [_dot_nt(pre[u][0], states[u]) for u in range(group)]
        for u in range(group):
            rows = pl.ds(pl.multiple_of((base + u) * c, c), c)
            o = inter[u] + pre[u][1]
            y = o * lax.rsqrt(jnp.mean(o * o, axis=-1, keepdims=True) + RMS_EPS) * nrm
            o_ref[0, rows, :] = (y * _silu(g_ref[0, rows, :].astype(F32))).astype(o_ref.dtype)
        return state_t

    lax.fori_loop(0, n_chunks // group, body, jnp.zeros((i_ref.shape[-1], dk), F32))


def _hgrn2(z, cols, lb, nrm, n_heads):
    b, s, _ = z.shape
    c = HG_CHUNK
    d = V7X_LANES
    cq, cf, ci, cg = cols
    sums = jnp.asarray(_hg_sum_matrix(c), BF16)
    masks = jnp.asarray(_hg_level_masks(c), F32)
    seq = lambda off: pl.BlockSpec((1, s, d), lambda bi, h: (bi, 0, off + h))
    return pl.pallas_call(
        functools.partial(_hg_kernel, c=c, n_chunks=s // c, group=min(HG_GROUP, s // c)),
        grid=(b, n_heads),
        in_specs=[seq(cq), seq(cf), seq(ci), seq(cg),
                  pl.BlockSpec((1, d), lambda bi, h: (0, h)),
                  pl.BlockSpec((1, d), lambda bi, h: (0, 0)),
                  pl.BlockSpec(sums.shape, lambda bi, h: (0, 0)),
                  pl.BlockSpec(masks.shape, lambda bi, h: (0, 0, 0))],
        out_specs=pl.BlockSpec((1, s, d), lambda bi, h: (bi, 0, h)),
        out_shape=jax.ShapeDtypeStruct((b, s, n_heads * d), BF16),
        compiler_params=_params("parallel", "parallel"),
        name="hgrn2",
    )(z, z, z, z, lb.astype(F32).reshape(1, n_heads * d), nrm.astype(F32).reshape(1, d), sums, masks)


RW_LOW = 256


def _rw_prep_kernel(*refs, first):
    if first:
        (zm_ref, zm_prev_ref, low_ref, low_prev_ref, mu_ref, mu_low_ref, lora_ref, w0_ref, a0_ref, v0_ref,
         kk_ref, ka_ref, r_out, lw_out, k_out, v_out, a_out, kk_out) = refs
        vf_ref = None
    else:
        (zm_ref, zm_prev_ref, low_ref, low_prev_ref, vf_ref, mu_ref, mu_low_ref, lora_ref, w0_ref, a0_ref,
         v0_ref, kk_ref, ka_ref, r_out, lw_out, k_out, v_out, a_out, kk_out) = refs
    i = pl.program_id(1)
    width = r_out.shape[-1]

    def lerp_shift(x, prev_block, mu):
        row = lax.broadcasted_iota(jnp.int32, x.shape, 0)
        prev = jnp.where(i > 0, prev_block[7:8, :], 0.0)
        shifted = jnp.where(row == 0, prev, pltpu.roll(x, 1, 0))
        return x + (shifted - x) * mu

    zm = lerp_shift(zm_ref[0].astype(F32), zm_prev_ref[0].astype(F32), mu_ref[...])
    low = lerp_shift(low_ref[0], low_prev_ref[0], mu_low_ref[...])
    lane = lax.broadcasted_iota(jnp.int32, low.shape, 1)
    low = jnp.where(lane < RW_W_RANK, jnp.tanh(low), low)
    lora = _dot(low.astype(BF16), lora_ref[...])
    r = zm[:, 0:width]
    k = zm[:, width:2 * width]
    v = zm[:, 2 * width:3 * width]
    x = -(w0_ref[...] + lora[:, 0:width])
    w_log = -(jnp.maximum(x, 0.0) + jnp.log1p(jnp.exp(-jnp.abs(x)))) - 0.5
    a = _sigmoid(a0_ref[...] + lora[:, width:2 * width])
    if not first:
        v = v + (vf_ref[0] - v) * _sigmoid(v0_ref[...] + lora[:, 2 * width:3 * width])
    r_out[0] = r
    lw_out[0] = -jnp.exp(w_log)
    k_out[0] = k * (1.0 + (a - 1.0) * ka_ref[...])
    v_out[0] = v
    a_out[0] = a
    kk_out[0] = k * kk_ref[...]


def _rw_chunk_kernel(r_ref, lw_ref, k_ref, v_ref, a_ref, kk_ref, g_ref, rk_ref, lng_ref, lnb_ref, tri_ref, bd_ref,
                     o_ref, state_ref, gp_ref, hq_ref, oraw_ref, *, c, n_chunks, n_pairs):
    hd = RW_HEAD

    @pl.when(pl.program_id(2) == 0)
    def _():
        state_ref[...] = jnp.zeros_like(state_ref)

    pw = 2 * hd
    lane = lax.broadcasted_iota(jnp.int32, (1, pw), 1)
    head_masks = ((lane < hd).astype(F32), (lane >= hd).astype(F32))
    in_h0 = lane < hd
    tri_inc_b = tri_ref[...]
    strict2 = bd_ref[0]
    incl2 = bd_ref[1]
    rr = lax.broadcasted_iota(jnp.int32, (pw, pw), 0)
    cc = lax.broadcasted_iota(jnp.int32, (pw, pw), 1)
    eye_p = rr == cc
    eye_f = eye_p.astype(F32)

    def head_sum(x):
        s0 = jnp.sum(x * head_masks[0], axis=-1, keepdims=True)
        s1 = jnp.sum(x * head_masks[1], axis=-1, keepdims=True)
        return jnp.where(in_h0, s0, s1)

    def stack_heads(x):
        return jnp.concatenate([x * head_masks[0], x * head_masks[1]], axis=0)

    def both(x):
        return x[:, :pw] + x[:, pw:]

    def transition(pi, ci):
        rows = pl.ds(pl.multiple_of(ci * c, c), c)
        lanes = slice(pi * pw, (pi + 1) * pw)
        r = r_ref[0, rows, lanes]
        lw = lw_ref[0, rows, lanes]
        k = k_ref[0, rows, lanes]
        v = v_ref[0, rows, lanes]
        a = a_ref[0, rows, lanes]
        kk = kk_ref[0, rows, lanes]
        hi, lo = _split_bf16(lw)
        g = both(_dot(tri_inc_b, jnp.concatenate([hi, lo], axis=1)))
        yield
        g_last = g[c - 1:c]
        e_in = jnp.exp(g)
        e_ex = jnp.exp(g - lw)
        e_neg = jnp.exp(-g)
        e_rev = jnp.exp(g_last - g)
        kkn = kk / jnp.maximum(jnp.sqrt(head_sum(kk * kk)), 1e-12)
        b_vec = kkn * a
        a2 = stack_heads(-kkn * e_ex).astype(BF16)
        r2 = stack_heads(r * e_in)
        v2 = stack_heads(v).astype(BF16)
        x = jnp.concatenate([a2, r2.astype(BF16)], axis=0)
        y = jnp.concatenate([b_vec * e_neg, b_vec * e_neg, k * e_neg, k * e_neg], axis=0).astype(BF16)
        sc = _dot_nt(x, y)
        yield
        l_ab = sc[:pw, :pw] * strict2
        l_ak = (sc[:pw, pw:] * strict2).astype(BF16)
        m_rb = (sc[pw:, :pw] * incl2).astype(BF16)
        m_rk = (sc[pw:, pw:] * incl2).astype(BF16)
        inv = eye_f + l_ab
        lb = l_ab.astype(BF16)
        power = _dot(lb, lb)
        lv = _dot(l_ak, v2).astype(BF16)
        yield
        for _ in range(int(math.log2(c)) - 2):
            pb = power.astype(BF16)
            res = _dot(jnp.concatenate([inv.astype(BF16), pb], axis=0), pb)
            yield
            inv = inv + res[:pw]
            power = res[pw:]
        inv_b = (inv + _dot(inv.astype(BF16), power.astype(BF16))).astype(BF16)
        yield
        tw = _dot(inv_b, jnp.concatenate([a2, lv], axis=1)).astype(BF16)
        yield
        gh = _dot(m_rb, tw)
        g2 = r2 + gh[:, :pw]
        h2 = gh[:, pw:] + _dot(m_rk, v2)
        pq = _dot_tn(stack_heads(b_vec * e_rev).astype(BF16), tw)
        yield
        p_mat = jnp.where(eye_p, jnp.exp(g_last), 0.0) + pq[:, :pw]
        q_mat = pq[:, pw:] + _dot_tn(stack_heads(k * e_rev).astype(BF16), v2)
        gp_ref[pi, ci, 0:c, :] = (g2[:c] + g2[c:]).astype(BF16)
        gp_ref[pi, ci, c:c + pw, :] = p_mat.astype(BF16)
        hq_ref[pi, ci, 0:c, :] = h2[:c] + h2[c:]
        hq_ref[pi, ci, c:c + pw, :] = q_mat

    def advance(pi):
        for ci in range(n_chunks):
            s_hi, s_lo = _split_bf16(state_ref[pi])
            res = both(_dot(gp_ref[pi, ci], jnp.concatenate([s_hi, s_lo], axis=1))) + hq_ref[pi, ci]
            oraw_ref[pi, ci * c:(ci + 1) * c, :] = res[:c]
            state_ref[pi] = res[c:]
            yield

    for pi in range(n_pairs + 1):
        work = [transition(pi, ci) for ci in range(n_chunks)] if pi < n_pairs else []
        if pi > 0:
            work.append(advance(pi - 1))
        _lock_step(work)

    for pi in range(n_pairs):
        lanes = slice(pi * pw, (pi + 1) * pw)
        o = oraw_ref[pi]
        mean = head_sum(o) * (1.0 / hd)
        cen = o - mean
        var = head_sum(cen * cen) * (1.0 / hd)
        o = cen * lax.rsqrt(var + RW_LN_EPS) * lng_ref[:, lanes] + lnb_ref[:, lanes]
        o = o + head_sum(r_ref[0, :, lanes] * k_ref[0, :, lanes] * rk_ref[:, lanes]) * v_ref[0, :, lanes]
        o_ref[0, :, lanes] = (o * _silu(g_ref[0, :, lanes].astype(F32))).astype(o_ref.dtype)


def _rwkv7(zm, zg, low, v_first, mu, w0, w2, a0, a2, v_mu, v0, v2, k_k, k_a, r_k, lnx_g, lnx_b,
           zm_col=0, zg_col=0):
    b, s, _ = zm.shape
    width = w0.shape[0]
    first = v_first is None
    tm = min(256, s)
    row = lambda x: x.astype(F32).reshape(1, -1)
    mu_low = jnp.concatenate([mu[3 * width:], v_mu if not first else jnp.zeros((RW_V_RANK,), F32),
                              jnp.zeros((RW_LOW - RW_W_RANK - RW_A_RANK - RW_V_RANK,), F32)])
    lora = jnp.zeros((RW_LOW, 3 * width), F32)
    lora = lora.at[0:RW_W_RANK, 0:width].set(w2)
    lora = lora.at[RW_W_RANK:RW_W_RANK + RW_A_RANK, width:2 * width].set(a2)
    if not first:
        lora = lora.at[RW_W_RANK + RW_A_RANK:RW_W_RANK + RW_A_RANK + RW_V_RANK, 2 * width:].set(v2)
        v0_row = row(v0)
    else:
        v0_row = jnp.zeros((1, width), F32)
    blk3 = 3 * width // V7X_LANES
    tile = lambda w_: pl.BlockSpec((1, tm, w_), lambda bi, i: (bi, i, 0))
    prev = lambda w_, col: pl.BlockSpec((1, 8, w_), lambda bi, i: (bi, jnp.maximum(i * (tm // 8) - 1, 0), col))
    full = lambda x: pl.BlockSpec(x.shape, lambda bi, i: (0,) * x.ndim)
    mu_row, mu_low_row, lora_b = row(mu[:3 * width]), row(mu_low), lora.astype(BF16)
    consts = [mu_row, mu_low_row, lora_b, row(w0), row(a0), v0_row, row(k_k), row(k_a)]
    assert zm_col % blk3 == 0
    in_specs = [pl.BlockSpec((1, tm, 3 * width), lambda bi, i: (bi, i, zm_col // blk3)),
                prev(3 * width, zm_col // blk3), tile(RW_LOW), prev(RW_LOW, 0)]
    args = [zm, zm, low, low]
    if not first:
        in_specs.append(tile(width))
        args.append(v_first)
    in_specs += [full(x) for x in consts]
    args += consts
    outs = pl.pallas_call(
        functools.partial(_rw_prep_kernel, first=first),
        grid=(b, s // tm),
        in_specs=in_specs,
        out_specs=[tile(width)] * 6,
        out_shape=[jax.ShapeDtypeStruct((b, s, width), F32)] * 6,
        compiler_params=_params("parallel", "parallel"),
        name="rwkv_prep",
    )(*args)
    r, lw, k, v, a, kk = outs

    c = RW_CHUNK
    blk = min(RW_BLOCK, s)
    pw = 2 * RW_HEAD
    n_pairs = width // pw
    tril = np.tril(np.ones((c, c), np.float32))
    tri = jnp.asarray(tril, BF16)
    two = np.eye(2, dtype=np.float32)
    bd = jnp.asarray(np.stack([np.kron(two, np.tril(tril, -1)), np.kron(two, tril)]))
    n_chunks = blk // c
    npg = min(RW_PAIRS, n_pairs)
    gw = npg * pw
    assert n_pairs % npg == 0 and (zg_col * V7X_LANES) % gw == 0
    seq = lambda off=0: pl.BlockSpec((1, blk, gw), lambda bi, p, i: (bi, i, off + p))
    par = pl.BlockSpec((1, gw), lambda bi, p, i: (0, p))
    y = pl.pallas_call(
        functools.partial(_rw_chunk_kernel, c=c, n_chunks=n_chunks, n_pairs=npg),
        grid=(b, n_pairs // npg, s // blk),
        in_specs=[seq()] * 6 + [seq(zg_col * V7X_LANES // gw), par, par, par,
                                pl.BlockSpec(tri.shape, lambda bi, p, i: (0, 0)),
                                pl.BlockSpec(bd.shape, lambda bi, p, i: (0, 0, 0))],
        out_specs=seq(),
        out_shape=jax.ShapeDtypeStruct((b, s, width), BF16),
        scratch_shapes=[pltpu.VMEM((npg, pw, pw), F32), pltpu.VMEM((npg, n_chunks, c + pw, pw), BF16),
                        pltpu.VMEM((npg, n_chunks, c + pw, pw), F32), pltpu.VMEM((npg, blk, pw), F32)],
        compiler_params=_params("parallel", "parallel", "arbitrary"),
        name="rwkv_chunk",
    )(r, lw, k, v, a, kk, zg, row(r_k), row(lnx_g), row(lnx_b), tri, bd)
    return y, (v if first else v_first)


def _merge_kernel(ya_ref, yb_ref, yc_ref, yd_ref, g0_ref, g1_ref, g2_ref, g3_ref, wb_ref, o_ref):
    acc = None
    for n, (y_ref, g_ref) in enumerate(zip((ya_ref, yb_ref, yc_ref, yd_ref), (g0_ref, g1_ref, g2_ref, g3_ref))):
        term = _sigmoid(g_ref[...].astype(F32)) * _dot(y_ref[...], wb_ref[n])
        acc = term if acc is None else acc + term
    o_ref[...] = acc.astype(o_ref.dtype)


def _merge(ys, z2d, gate_col, wb):
    m, w = ys[0].shape
    d = wb.shape[-1]
    tm = min(256, m)
    y_spec = pl.BlockSpec((tm, w), lambda i: (i, 0))
    g_spec = lambda n: pl.BlockSpec((tm, d), lambda i: (i, gate_col + n))
    return pl.pallas_call(
        _merge_kernel,
        grid=(m // tm,),
        in_specs=[y_spec] * N_BRANCH + [g_spec(n) for n in range(N_BRANCH)]
        + [pl.BlockSpec(wb.shape, lambda i: (0, 0, 0))],
        out_specs=pl.BlockSpec((tm, d), lambda i: (i, 0)),
        out_shape=jax.ShapeDtypeStruct((m, d), BF16),
        compiler_params=_params("parallel"),
        name="merge",
    )(*ys, z2d, z2d, z2d, z2d, wb)


def _out_ln_kernel(m_ref, w_ref, h_ref, g_ref, b_ref, o_ref, ob_ref, *, alpha):
    x = alpha * h_ref[...] + _dot(m_ref[...], w_ref[...])
    mu = jnp.mean(x, axis=-1, keepdims=True)
    cen = x - mu
    var = jnp.mean(cen * cen, axis=-1, keepdims=True)
    y = cen * lax.rsqrt(var + LN_EPS) * g_ref[...] + b_ref[...]
    o_ref[...] = y
    ob_ref[...] = y.astype(ob_ref.dtype)


def _out_ln(merged, w_out, h, ln_g, ln_b, alpha):
    m, d = h.shape
    tm = min(512, m)
    tile = pl.BlockSpec((tm, d), lambda i: (i, 0))
    vec = pl.BlockSpec((1, d), lambda i: (0, 0))
    return pl.pallas_call(
        functools.partial(_out_ln_kernel, alpha=alpha),
        grid=(m // tm,),
        in_specs=[tile, pl.BlockSpec(w_out.shape, lambda i: (0, 0)), tile, vec, vec],
        out_specs=[tile, tile],
        out_shape=[jax.ShapeDtypeStruct((m, d), F32), jax.ShapeDtypeStruct((m, d), BF16)],
        compiler_params=_params("parallel"),
        name="out_ln",
    )(merged, w_out, h, ln_g.astype(F32).reshape(1, d), ln_b.astype(F32).reshape(1, d))


def kernel(x, w_in, rel_bias, da_lambda, da_subln, hg_lower, hg_norm, rw_mu, rw_w0, rw_w2, rw_a0, rw_a2, rw_v1,
           rw_v_mu, rw_v0, rw_v2, rw_kk, rw_ka, rw_rk, rw_lnx_g, rw_lnx_b, w_branch, w_out, ln_g, ln_b):
    b, s, d = x.shape
    depth = w_in.shape[0]
    width = w_branch.shape[2]
    heads = width // V7X_LANES
    m = b * s
    alpha = (2 * depth) ** 0.25
    gb = width // V7X_LANES

    off_a, off_b, off_c = 0, 4 * width, 8 * width
    rw_mix = 3 * width + RW_W_RANK + RW_A_RANK
    off_cg = off_c + rw_mix
    off_d = off_cg + width
    off_m = off_d + 4 * width
    col_c, col_cg, col_a, col_b, col_d = 0, 3 * gb, 4 * gb, 8 * gb, 12 * gb
    col_m = 16 * gb

    lbs = jnp.cumsum(jax.nn.softmax(hg_lower.astype(F32), axis=0), axis=0)
    lbs = lbs - lbs[0:1]

    h = x.reshape(m, d)
    hb = h.astype(BF16)
    v_first = None
    for l in range(depth):
        wl = w_in[l]
        w_main = jnp.concatenate([wl[:, off_c:off_c + 3 * width], wl[:, off_cg:off_cg + width],
                                  wl[:, off_a:off_a + 4 * width], wl[:, off_b:off_b + 4 * width],
                                  wl[:, off_d:off_d + 4 * width], wl[:, off_m:]], axis=1).astype(BF16)
        low_cols = [wl[:, off_c + 3 * width:off_cg]]
        if l > 0:
            low_cols.append(rw_v1[l - 1])
        n_low = sum(c.shape[1] for c in low_cols)
        low_cols.append(jnp.zeros((d, RW_LOW - n_low), F32))
        w_low = jnp.concatenate(low_cols, axis=1).astype(BF16)

        z2d = _matmul(hb, w_main, BF16, min(1024, m), 1024, "in_proj")
        low = _matmul(hb, w_low, F32, min(1024, m), RW_LOW, "in_proj_low").reshape(b, s, RW_LOW)
        z = z2d.reshape(b, s, -1)

        y_a = _diff_attention(z, (col_a, col_a + gb, col_a + 2 * gb, col_a + 3 * gb), rel_bias, da_lambda[l],
                              da_subln[l], l, heads)
        y_b = _hgrn2(z, (col_b, col_b + gb, col_b + 2 * gb, col_b + 3 * gb), lbs[l], hg_norm[l], heads)
        y_c, v_first = _rwkv7(z, z, low, v_first, rw_mu[l], rw_w0[l], rw_w2[l], rw_a0[l], rw_a2[l],
                              rw_v_mu[l - 1] if l > 0 else None, rw_v0[l - 1] if l > 0 else None,
                              rw_v2[l - 1] if l > 0 else None, rw_kk[l], rw_ka[l], rw_rk[l], rw_lnx_g[l],
                              rw_lnx_b[l], zm_col=col_c, zg_col=col_cg)
        y_d = _stick_breaking(z, (col_d, col_d + gb, col_d + 2 * gb, col_d + 3 * gb), heads, V7X_LANES)

        merged = _merge([y.reshape(m, width) for y in (y_a, y_b, y_c, y_d)], z2d, col_m * V7X_LANES // d,
                        w_branch[l].astype(BF16))
        h, hb = _out_ln(merged, w_out[l].astype(BF16), h, ln_g[l], ln_b[l], alpha)
    return h.reshape(b, s, d)
```

```python
import functools
import math

import numpy as np
import jax
import jax.numpy as jnp
from jax import lax
from jax.experimental import pallas as pl
from jax.experimental.pallas import tpu as pltpu

F32 = jnp.float32
BF16 = jnp.bfloat16

V7X_LANES = 128
V7X_VMEM_LIMIT_BYTES = 56 * 1024 * 1024

N_BRANCH = 4
DA_QK = 64
REL_BUCKETS = 32
REL_MAX_DIST = 128
HG_CHUNK = 64
RW_HEAD = 64
RW_W_RANK = 64
RW_A_RANK = 64
RW_V_RANK = 32
RW_LN_EPS = 64e-5
RW_CHUNK = 64
LN_EPS = 1e-5
RMS_EPS = 1e-6
NEG_BIG = -1e30

ATT_TILE = 512
ATT_STRIP = 64
HG_GROUP = 8
RW_BLOCK = 512
RW_PAIRS = 4


def _params(*sem):
    return pltpu.CompilerParams(dimension_semantics=sem, vmem_limit_bytes=V7X_VMEM_LIMIT_BYTES)


def _dot(a, b):
    return jnp.dot(a, b, preferred_element_type=F32)


def _dot_nt(a, b):
    return lax.dot_general(a, b, (((1,), (1,)), ((), ())), preferred_element_type=F32)


def _dot_tn(a, b):
    return lax.dot_general(a, b, (((0,), (0,)), ((), ())), preferred_element_type=F32)


def _sigmoid(x):
    return 1.0 / (1.0 + jnp.exp(-x))


def _silu(x):
    return x * _sigmoid(x)


def _split_bf16(x):
    hi = x.astype(BF16)
    lo = (x - hi.astype(F32)).astype(BF16)
    return hi, lo


def _lock_step(generators):
    live = list(generators)
    while live:
        for gen in list(live):
            try:
                next(gen)
            except StopIteration:
                live.remove(gen)


def _matmul_kernel(x_ref, w_ref, o_ref):
    o_ref[...] = _dot(x_ref[...], w_ref[...]).astype(o_ref.dtype)


def _matmul(x, w, out_dtype, tm, tn, name):
    m, k = x.shape
    n = w.shape[1]
    return pl.pallas_call(
        _matmul_kernel,
        grid=(m // tm, n // tn),
        in_specs=[pl.BlockSpec((tm, k), lambda i, j: (i, 0)),
                  pl.BlockSpec((k, tn), lambda i, j: (0, j))],
        out_specs=pl.BlockSpec((tm, tn), lambda i, j: (i, j)),
        out_shape=jax.ShapeDtypeStruct((m, n), out_dtype),
        compiler_params=_params("parallel", "arbitrary"),
        name=name,
    )(x, w)


def _t5_bucket_np(dist):
    max_exact = REL_BUCKETS // 2
    n = np.maximum(dist, 0)
    nf = np.maximum(n, 1).astype(np.float32)
    large = max_exact + (np.log(nf / np.float32(max_exact)) / np.float32(math.log(REL_MAX_DIST / max_exact))
                         * np.float32(REL_BUCKETS - max_exact)).astype(np.int32)
    large = np.minimum(large, REL_BUCKETS - 1)
    return np.where(n < max_exact, n, large)


def _bias_tiles(rel_bias, t):
    assert t + 1 >= REL_MAX_DIST
    rows = np.arange(t)[:, None]
    cols = np.arange(t)[None, :]
    dist = np.stack([behind * t + rows - cols for behind in range(3)])
    bucket = jnp.asarray(_t5_bucket_np(dist).astype(np.int32))
    table = rel_bias.astype(F32).T * math.log2(math.e)
    n_heads = table.shape[0]
    tiles = jnp.zeros((n_heads,) + dist.shape, F32)
    for bkt in range(REL_BUCKETS):
        tiles = jnp.where(bucket[None] == bkt, table[:, bkt][:, None, None, None], tiles)
    tiles = jnp.where(jnp.asarray(dist >= 0)[None], tiles, NEG_BIG)
    return jnp.concatenate([tiles, jnp.full((n_heads, 1, t, t), NEG_BIG, F32)], axis=1)


def _causal_walk(nq, depth, n_kinds):
    steps = [(i, i - jj, min(jj, n_kinds - 2), int(jj == 0), int(jj == i)) for i in range(nq) for jj in range(i + 1)]
    pad = (steps[-1][0], steps[-1][1], n_kinds - 1, 0, 0)
    return np.asarray(steps + [pad] * (depth - 1), np.int32).T


def _da_kernel(tab_ref, lam_ref, q_ref, k_ref, v_ref, g_ref, bias_ref, subln_ref, o_ref,
               qm_ref, s_ref, p_ref, alpha_ref, m_ref, acc_ref, *, lam_init, t, n_steps, strip):
    w = 2 * DA_QK
    lane = lax.broadcasted_iota(jnp.int32, q_ref.shape[1:], 1)
    q = (q_ref[0].astype(F32) * (DA_QK ** -0.5 * math.log2(math.e))).astype(BF16)
    zero = jnp.zeros_like(q)
    qm_ref[0] = jnp.where(lane < DA_QK, q, zero)
    qm_ref[1] = jnp.where(lane >= DA_QK, q, zero)
    s_ref[...] = jnp.zeros(s_ref.shape, F32)
    p_ref[...] = jnp.zeros(p_ref.shape, BF16)
    alpha_ref[...] = jnp.ones(alpha_ref.shape, F32)
    m_ref[...] = jnp.zeros(m_ref.shape, F32)
    acc_ref[...] = jnp.zeros(acc_ref.shape, F32)
    ones = jnp.ones((t, w), BF16)
    lam = lam_ref[...]
    lam_full = (jnp.exp(jnp.sum(lam[0:1] * lam[1:2], axis=-1, keepdims=True))
                - jnp.exp(jnp.sum(lam[2:3] * lam[3:4], axis=-1, keepdims=True)) + lam_init)
    dead = bias_ref.shape[1] - 1

    def tile_rows(idx):
        return pl.ds(pl.multiple_of(idx * t, t), t)

    def stages(it, half):
        c = jnp.maximum(it - 2, 0)
        slot = tab_ref[0, c] % 2
        v_ext = jnp.concatenate([v_ref[0, tile_rows(tab_ref[1, c]), :], ones], axis=1)
        for m_i in range(2):
            acc_ref[slot, m_i] = (jnp.tile(alpha_ref[half, m_i], (1, 2)) * acc_ref[slot, m_i]
                                  + _dot(p_ref[half, m_i], v_ext))

        b = jnp.maximum(it - 1, 0)
        kind = jnp.where(it >= 1, tab_ref[2, b], dead)
        first = jnp.logical_and(it >= 1, tab_ref[3, b] == 1)
        for r0 in range(0, t, strip):
            rows = pl.ds(r0, strip)
            bias = bias_ref[0, kind, rows, :]
            for m_i in range(2):
                s = s_ref[1 - half, m_i, rows, :] + bias
                m_prev = jnp.where(first, NEG_BIG, m_ref[m_i, rows, :])
                m_new = jnp.maximum(m_prev, jnp.max(s, axis=-1, keepdims=True))
                p_ref[1 - half, m_i, rows, :] = jnp.exp2(s - jnp.tile(m_new, (1, t // w))).astype(BF16)
                alpha_ref[1 - half, m_i, rows, :] = jnp.exp2(m_prev - m_new)
                m_ref[m_i, rows, :] = m_new

        kj = k_ref[0, tile_rows(tab_ref[1, it]), :]
        q_rows = tile_rows(tab_ref[0, it])
        for m_i in range(2):
            s_ref[half, m_i] = _dot_nt(qm_ref[m_i, q_rows, :], kj)

    def finish(it):
        c = jnp.maximum(it - 2, 0)

        @pl.when(jnp.logical_and(it >= 2, tab_ref[4, c] == 1))
        def _():
            rows = tile_rows(tab_ref[0, c])
            slot = tab_ref[0, c] % 2
            o = (acc_ref[slot, 0, :, 0:w] / acc_ref[slot, 0, :, w:2 * w]
                 - lam_full * (acc_ref[slot, 1, :, 0:w] / acc_ref[slot, 1, :, w:2 * w]))
            o = o * lax.rsqrt(jnp.mean(o * o, axis=-1, keepdims=True) + RMS_EPS) * subln_ref[...] * (1.0 - lam_init)
            o_ref[0, rows, :] = (o * _silu(g_ref[0, rows, :].astype(F32))).astype(o_ref.dtype)

    def body(i2, carry):
        for half in range(2):
            stages(2 * i2 + half, half)
        for half in range(2):
            finish(2 * i2 + half)
        return carry

    lax.fori_loop(0, (n_steps + 3) // 2, body, 0)


def _diff_attention(z, cols, bias_tiles, lam, subln, layer, n_heads):
    b, s, _ = z.shape
    t = bias_tiles.shape[-1]
    lam_init = 0.8 - 0.6 * math.exp(-0.3 * layer)
    cq, ck, cv, cg = cols
    w = 2 * DA_QK
    tab = _causal_walk(s // t, 4, 4)
    n_steps = tab.shape[1] - 3
    seq = lambda off: pl.BlockSpec((1, s, w), lambda bi, h, tab_: (bi, 0, off + h))
    return pl.pallas_call(
        functools.partial(_da_kernel, lam_init=lam_init, t=t, n_steps=n_steps, strip=min(ATT_STRIP, t)),
        grid_spec=pltpu.PrefetchScalarGridSpec(
            num_scalar_prefetch=1,
            grid=(b, n_heads),
            in_specs=[
                pl.BlockSpec((4, DA_QK), lambda bi, h, tab_: (0, 0)),
                seq(cq), seq(ck), seq(cv), seq(cg),
                pl.BlockSpec((1, 4, t, t), lambda bi, h, tab_: (h, 0, 0, 0)),
                pl.BlockSpec((1, w), lambda bi, h, tab_: (0, 0)),
            ],
            out_specs=pl.BlockSpec((1, s, w), lambda bi, h, tab_: (bi, 0, h)),
            scratch_shapes=[pltpu.VMEM((2, s, w), BF16), pltpu.VMEM((2, 2, t, t), F32),
                            pltpu.VMEM((2, 2, t, t), BF16), pltpu.VMEM((2, 2, t, w), F32),
                            pltpu.VMEM((2, t, w), F32), pltpu.VMEM((2, 2, t, 2 * w), F32)],
        ),
        out_shape=jax.ShapeDtypeStruct((b, s, n_heads * w), BF16),
        compiler_params=_params("parallel", "parallel"),
        name="diff_attention",
    )(jnp.asarray(tab), lam.astype(F32), z, z, z, z, bias_tiles, subln.astype(F32).reshape(1, w))


SB_SUB = 256


def _sb_kernel(tab_ref, q_ref, k_ref, v_ref, g_ref, tri_ref, neg_ref, o_ref,
               qs_ref, s_ref, z_ref, lk_ref, later_ref, p_ref, acc_ref, *, t, d, n_steps):
    n_sub = t // SB_SUB
    qs_ref[...] = (q_ref[0].astype(F32) * (d ** -0.5 * math.log2(math.e))).astype(BF16)
    s_ref[...] = jnp.zeros(s_ref.shape, F32)
    z_ref[...] = jnp.full(z_ref.shape, NEG_BIG, F32)
    lk_ref[...] = jnp.zeros(lk_ref.shape, BF16)
    later_ref[...] = jnp.zeros(later_ref.shape, F32)
    p_ref[...] = jnp.zeros(p_ref.shape, BF16)
    acc_ref[...] = jnp.zeros(acc_ref.shape, F32)
    tri = tri_ref[...]
    dead = neg_ref.shape[0] - 1

    def tile_rows(idx):
        return pl.ds(pl.multiple_of(idx * t, t), t)

    def stages(it, half):
        c3 = jnp.maximum(it - 2, 0)
        later = later_ref[...] * jnp.where(jnp.logical_and(it >= 2, tab_ref[3, c3] == 1), 0.0, 1.0)
        for c in reversed(range(n_sub)):
            cols = slice(c * SB_SUB, (c + 1) * SB_SUB)
            sums = _dot(lk_ref[half, :, cols], tri)
            p_ref[half, :, cols] = jnp.exp2(z_ref[half, :, cols] + sums
                                            + jnp.tile(later, (1, SB_SUB // d))).astype(BF16)
            later = later + sums[:, 0:1]
        later_ref[...] = later

        s_ref[half] = _dot_nt(qs_ref[tile_rows(tab_ref[0, it]), :], k_ref[0, tile_rows(tab_ref[1, it]), :])

        c4 = jnp.maximum(it - 3, 0)
        slot = tab_ref[0, c4] % 2
        acc_ref[slot] += _dot(p_ref[1 - half], v_ref[0, tile_rows(tab_ref[1, c4]), :])

        b2 = jnp.maximum(it - 1, 0)
        zs = s_ref[1 - half] + neg_ref[jnp.where(it >= 1, tab_ref[2, b2], dead)]
        neg_z = -zs
        soft = jnp.log(1.0 + jnp.exp2(jnp.minimum(zs, neg_z))) * math.log2(math.e)
        z_ref[1 - half] = zs
        lk_ref[1 - half] = (jnp.minimum(neg_z, 0.0) - soft).astype(BF16)

    def finish(it):
        c4 = jnp.maximum(it - 3, 0)

        @pl.when(jnp.logical_and(it >= 3, tab_ref[4, c4] == 1))
        def _():
            rows = tile_rows(tab_ref[0, c4])
            slot = tab_ref[0, c4] % 2
            o_ref[0, rows, :] = (acc_ref[slot] * _silu(g_ref[0, rows, :].astype(F32))).astype(o_ref.dtype)
            acc_ref[slot] = jnp.zeros(acc_ref.shape[1:], F32)

    def body(i2, carry):
        for half in range(2):
            stages(2 * i2 + half, half)
        for half in range(2):
            finish(2 * i2 + half)
        return carry

    lax.fori_loop(0, (n_steps + 4) // 2, body, 0)


def _stick_breaking(z, cols, n_heads, d):
    b, s, _ = z.shape
    t = min(ATT_TILE, s)
    cq, ck, cv, cg = cols
    tab = _causal_walk(s // t, 5, 3)
    n_steps = tab.shape[1] - 4
    tri = jnp.asarray(np.tril(np.ones((SB_SUB, SB_SUB), np.float32)), BF16)
    strict = np.tril(np.ones((t, t), np.float32), -1)
    neg = jnp.asarray(np.stack([(strict - 1.0), np.zeros_like(strict), -np.ones_like(strict)]) * -NEG_BIG)
    seq = lambda off: pl.BlockSpec((1, s, d), lambda bi, h, tab_: (bi, 0, off + h))
    const = lambda x: pl.BlockSpec(x.shape, lambda bi, h, tab_: (0,) * x.ndim)
    return pl.pallas_call(
        functools.partial(_sb_kernel, t=t, d=d, n_steps=n_steps),
        grid_spec=pltpu.PrefetchScalarGridSpec(
            num_scalar_prefetch=1,
            grid=(b, n_heads),
            in_specs=[seq(cq), seq(ck), seq(cv), seq(cg), const(tri), const(neg)],
            out_specs=pl.BlockSpec((1, s, d), lambda bi, h, tab_: (bi, 0, h)),
            scratch_shapes=[pltpu.VMEM((s, d), BF16), pltpu.VMEM((2, t, t), F32), pltpu.VMEM((2, t, t), F32),
                            pltpu.VMEM((2, t, t), BF16), pltpu.VMEM((t, d), F32), pltpu.VMEM((2, t, t), BF16),
                            pltpu.VMEM((2, t, d), F32)],
        ),
        out_shape=jax.ShapeDtypeStruct((b, s, n_heads * d), BF16),
        compiler_params=_params("parallel", "parallel"),
        name="stick_breaking",
    )(jnp.asarray(tab), z, z, z, z, tri, neg)


def _hg_level_sizes(c):
    sizes = []
    m = c // 2
    while m >= 1:
        sizes.append(m)
        m //= 2
    return sizes


def _hg_sum_matrix(c):
    r = np.arange(c)[:, None]
    s = np.arange(c)[None, :]
    blocks = []
    for m in _hg_level_sizes(c):
        start = (r // m) * m
        upper = (r // m) % 2 == 1
        lower_part = (s > r) & (s <= start + m - 1)
        blocks.append(np.where(upper, (s >= start) & (s <= r), lower_part))
    blocks.append(s <= r)
    blocks.append(s > r)
    return np.concatenate(blocks, 0).astype(np.float32)


def _hg_level_masks(c):
    r = np.arange(c)[:, None]
    s = np.arange(c)[None, :]
    masks = [((r // m) % 2 == 1) & ((s // m) == (r // m) - 1) for m in _hg_level_sizes(c)]
    masks.append(r == s)
    return np.stack(masks, 0).astype(np.float32)


def _hg_kernel(q_ref, f_ref, i_ref, g_ref, lb_ref, nrm_ref, sums_ref, masks_ref, o_ref, *, c, n_chunks, group):
    n_levels = len(_hg_level_sizes(c))
    lb = lb_ref[...]
    log_lb = jnp.log(lb)
    log_1m_lb = jnp.log1p(-lb)
    sums = sums_ref[...]
    nrm = nrm_ref[...]
    dk = q_ref.shape[-1]

    def chunk_pair(first, slot, out):
        rows = pl.ds(pl.multiple_of(first * c, 2 * c), 2 * c)
        q2 = q_ref[0, rows, :].astype(F32)
        zf = f_ref[0, rows, :].astype(F32)
        inp2 = i_ref[0, rows, :]
        e = jnp.exp(-jnp.abs(zf))
        log_sig = jnp.minimum(zf, 0.0) - jnp.log1p(e)
        arg = log_1m_lb + log_sig
        log_f = jnp.maximum(log_lb, arg) + jnp.log1p(jnp.exp(-jnp.abs(log_lb - arg)))
        k2 = (1.0 - lb) * jnp.where(zf >= 0.0, e, 1.0) / (1.0 + e)
        lf = jnp.concatenate([log_f[:c], log_f[c:]], axis=1).astype(BF16)
        d_cat = jnp.exp(_dot(sums, lf))
        yield
        halves = []
        for u in range(2):
            q = q2[u * c:(u + 1) * c]
            k_in = k2[u * c:(u + 1) * c]
            d_all = d_cat[:, u * dk:(u + 1) * dk]
            prods = [_dot_nt(q.astype(BF16), k_in.astype(BF16))]
            for lv in range(n_levels):
                x = d_all[lv * c:(lv + 1) * c]
                prods.append(_dot_nt((q * x).astype(BF16), (k_in * x).astype(BF16)))
            halves.append((q, k_in, d_all, prods))
        yield
        for u, (q, k_in, d_all, prods) in enumerate(halves):
            scores = masks_ref[n_levels] * prods[0]
            for lv in range(n_levels):
                scores = scores + masks_ref[lv] * prods[lv + 1]
            eb = d_all[n_levels * c:(n_levels + 1) * c]
            rev = d_all[(n_levels + 1) * c:(n_levels + 2) * c]
            inp = inp2[u * c:(u + 1) * c]
            out[slot + u] = ((q * eb).astype(BF16), _dot(scores.astype(BF16), inp), eb[c - 1:c],
                             _dot_tn(inp, (k_in * rev).astype(BF16)))
        yield

    def body(gi, state_t):
        base = gi * group
        pre = {}
        _lock_step([chunk_pair(base + u, u, pre) for u in range(0, group, 2)])
        states = []
        for u in range(group):
            _, _, decay, update = pre[u]
            states.append(state_t.astype(BF16))
            state_t = state_t * decay + update
        inter = [_dot_nt(pre[u][0], states[u]) for u in range(group)]
        for u in range(group):
            rows = pl.ds(pl.multiple_of((base + u) * c, c), c)
            o = inter[u] + pre[u][1]
            y = o * lax.rsqrt(jnp.mean(o * o, axis=-1, keepdims=True) + RMS_EPS) * nrm
            o_ref[0, rows, :] = (y * _silu(g_ref[0, rows, :].astype(F32))).astype(o_ref.dtype)
        return state_t

    lax.fori_loop(0, n_chunks // group, body, jnp.zeros((i_ref.shape[-1], dk), F32))


def _hgrn2(z, cols, lb, nrm, n_heads):
    b, s, _ = z.shape
    c = HG_CHUNK
    d = V7X_LANES
    cq, cf, ci, cg = cols
    sums = jnp.asarray(_hg_sum_matrix(c), BF16)
    masks = jnp.asarray(_hg_level_masks(c), F32)
    seq = lambda off: pl.BlockSpec((1, s, d), lambda bi, h: (bi, 0, off + h))
    return pl.pallas_call(
        functools.partial(_hg_kernel, c=c, n_chunks=s // c, group=min(HG_GROUP, s // c)),
        grid=(b, n_heads),
        in_specs=[seq(cq), seq(cf), seq(ci), seq(cg),
                  pl.BlockSpec((1, d), lambda bi, h: (0, h)),
                  pl.BlockSpec((1, d), lambda bi, h: (0, 0)),
                  pl.BlockSpec(sums.shape, lambda bi, h: (0, 0)),
                  pl.BlockSpec(masks.shape, lambda bi, h: (0, 0, 0))],
        out_specs=pl.BlockSpec((1, s, d), lambda bi, h: (bi, 0, h)),
        out_shape=jax.ShapeDtypeStruct((b, s, n_heads * d), BF16),
        compiler_params=_params("parallel", "parallel"),
        name="hgrn2",
    )(z, z, z, z, lb.astype(F32).reshape(1, n_heads * d), nrm.astype(F32).reshape(1, d), sums, masks)


RW_LOW = 256


def _rw_prep_kernel(*refs, first):
    if first:
        (zm_ref, zm_prev_ref, low_ref, low_prev_ref, mu_ref, mu_low_ref, lora_ref, w0_ref, a0_ref, v0_ref,
         kk_ref, ka_ref, r_out, lw_out, k_out, v_out, a_out, kk_out) = refs
        vf_ref = None
    else:
        (zm_ref, zm_prev_ref, low_ref, low_prev_ref, vf_ref, mu_ref, mu_low_ref, lora_ref, w0_ref, a0_ref,
         v0_ref, kk_ref, ka_ref, r_out, lw_out, k_out, v_out, a_out, kk_out) = refs
    i = pl.program_id(1)
    width = r_out.shape[-1]

    def lerp_shift(x, prev_block, mu):
        row = lax.broadcasted_iota(jnp.int32, x.shape, 0)
        prev = jnp.where(i > 0, prev_block[7:8, :], 0.0)
        shifted = jnp.where(row == 0, prev, pltpu.roll(x, 1, 0))
        return x + (shifted - x) * mu

    zm = lerp_shift(zm_ref[0].astype(F32), zm_prev_ref[0].astype(F32), mu_ref[...])
    low = lerp_shift(low_ref[0], low_prev_ref[0], mu_low_ref[...])
    lane = lax.broadcasted_iota(jnp.int32, low.shape, 1)
    low = jnp.where(lane < RW_W_RANK, jnp.tanh(low), low)
    lora = _dot(low.astype(BF16), lora_ref[...])
    r = zm[:, 0:width]
    k = zm[:, width:2 * width]
    v = zm[:, 2 * width:3 * width]
    x = -(w0_ref[...] + lora[:, 0:width])
    w_log = -(jnp.maximum(x, 0.0) + jnp.log1p(jnp.exp(-jnp.abs(x)))) - 0.5
    a = _sigmoid(a0_ref[...] + lora[:, width:2 * width])
    if not first:
        v = v + (vf_ref[0] - v) * _sigmoid(v0_ref[...] + lora[:, 2 * width:3 * width])
    r_out[0] = r
    lw_out[0] = -jnp.exp(w_log)
    k_out[0] = k * (1.0 + (a - 1.0) * ka_ref[...])
    v_out[0] = v
    a_out[0] = a
    kk_out[0] = k * kk_ref[...]


def _rw_chunk_kernel(r_ref, lw_ref, k_ref, v_ref, a_ref, kk_ref, g_ref, rk_ref, lng_ref, lnb_ref, tri_ref, bd_ref,
                     o_ref, state_ref, gp_ref, hq_ref, oraw_ref, *, c, n_chunks, n_pairs):
    hd = RW_HEAD

    @pl.when(pl.program_id(2) == 0)
    def _():
        state_ref[...] = jnp.zeros_like(state_ref)

    pw = 2 * hd
    lane = lax.broadcasted_iota(jnp.int32, (1, pw), 1)
    head_masks = ((lane < hd).astype(F32), (lane >= hd).astype(F32))
    in_h0 = lane < hd
    tri_inc_b = tri_ref[...]
    strict2 = bd_ref[0]
    incl2 = bd_ref[1]
    rr = lax.broadcasted_iota(jnp.int32, (pw, pw), 0)
    cc = lax.broadcasted_iota(jnp.int32, (pw, pw), 1)
    eye_p = rr == cc
    eye_f = eye_p.astype(F32)

    def head_sum(x):
        s0 = jnp.sum(x * head_masks[0], axis=-1, keepdims=True)
        s1 = jnp.sum(x * head_masks[1], axis=-1, keepdims=True)
        return jnp.where(in_h0, s0, s1)

    def stack_heads(x):
        return jnp.concatenate([x * head_masks[0], x * head_masks[1]], axis=0)

    def both(x):
        return x[:, :pw] + x[:, pw:]

    def transition(pi, ci):
        rows = pl.ds(pl.multiple_of(ci * c, c), c)
        lanes = slice(pi * pw, (pi + 1) * pw)
        r = r_ref[0, rows, lanes]
        lw = lw_ref[0, rows, lanes]
        k = k_ref[0, rows, lanes]
        v = v_ref[0, rows, lanes]
        a = a_ref[0, rows, lanes]
        kk = kk_ref[0, rows, lanes]
        hi, lo = _split_bf16(lw)
        g = both(_dot(tri_inc_b, jnp.concatenate([hi, lo], axis=1)))
        yield
        g_last = g[c - 1:c]
        e_in = jnp.exp(g)
        e_ex = jnp.exp(g - lw)
        e_neg = jnp.exp(-g)
        e_rev = jnp.exp(g_last - g)
        kkn = kk / jnp.maximum(jnp.sqrt(head_sum(kk * kk)), 1e-12)
        b_vec = kkn * a
        a2 = stack_heads(-kkn * e_ex).astype(BF16)
        r2 = stack_heads(r * e_in)
        v2 = stack_heads(v).astype(BF16)
        x = jnp.concatenate([a2, r2.astype(BF16)], axis=0)
        y = jnp.concatenate([b_vec * e_neg, b_vec * e_neg, k * e_neg, k * e_neg], axis=0).astype(BF16)
        sc = _dot_nt(x, y)
        yield
        l_ab = sc[:pw, :pw] * strict2
        l_ak = (sc[:pw, pw:] * strict2).astype(BF16)
        m_rb = (sc[pw:, :pw] * incl2).astype(BF16)
        m_rk = (sc[pw:, pw:] * incl2).astype(BF16)
        inv = eye_f + l_ab
        lb = l_ab.astype(BF16)
        power = _dot(lb, lb)
        lv = _dot(l_ak, v2).astype(BF16)
        yield
        for _ in range(int(math.log2(c)) - 2):
            pb = power.astype(BF16)
            res = _dot(jnp.concatenate([inv.astype(BF16), pb], axis=0), pb)
            yield
            inv = inv + res[:pw]
            power = res[pw:]
        inv_b = (inv + _dot(inv.astype(BF16), power.astype(BF16))).astype(BF16)
        yield
        tw = _dot(inv_b, jnp.concatenate([a2, lv], axis=1)).astype(BF16)
        yield
        gh = _dot(m_rb, tw)
        g2 = r2 + gh[:, :pw]
        h2 = gh[:, pw:] + _dot(m_rk, v2)
        pq = _dot_tn(stack_heads(b_vec * e_rev).astype(BF16), tw)
        yield
        p_mat = jnp.where(eye_p, jnp.exp(g_last), 0.0) + pq[:, :pw]
        q_mat = pq[:, pw:] + _dot_tn(stack_heads(k * e_rev).astype(BF16), v2)
        gp_ref[pi, ci, 0:c, :] = (g2[:c] + g2[c:]).astype(BF16)
        gp_ref[pi, ci, c:c + pw, :] = p_mat.astype(BF16)
        hq_ref[pi, ci, 0:c, :] = h2[:c] + h2[c:]
        hq_ref[pi, ci, c:c + pw, :] = q_mat

    def advance(pi):
        for ci in range(n_chunks):
            s_hi, s_lo = _split_bf16(state_ref[pi])
            res = both(_dot(gp_ref[pi, ci], jnp.concatenate([s_hi, s_lo], axis=1))) + hq_ref[pi, ci]
            oraw_ref[pi, ci * c:(ci + 1) * c, :] = res[:c]
            state_ref[pi] = res[c:]
            yield

    for pi in range(n_pairs + 1):
        work = [transition(pi, ci) for ci in range(n_chunks)] if pi < n_pairs else []
        if pi > 0:
            work.append(advance(pi - 1))
        _lock_step(work)

    for pi in range(n_pairs):
        lanes = slice(pi * pw, (pi + 1) * pw)
        o = oraw_ref[pi]
        mean = head_sum(o) * (1.0 / hd)
        cen = o - mean
        var = head_sum(cen * cen) * (1.0 / hd)
        o = cen * lax.rsqrt(var + RW_LN_EPS) * lng_ref[:, lanes] + lnb_ref[:, lanes]
        o = o + head_sum(r_ref[0, :, lanes] * k_ref[0, :, lanes] * rk_ref[:, lanes]) * v_ref[0, :, lanes]
        o_ref[0, :, lanes] = (o * _silu(g_ref[0, :, lanes].astype(F32))).astype(o_ref.dtype)


def _rwkv7(zm, zg, low, v_first, mu, w0, w2, a0, a2, v_mu, v0, v2, k_k, k_a, r_k, lnx_g, lnx_b,
           zm_col=0, zg_col=0):
    b, s, _ = zm.shape
    width = w0.shape[0]
    first = v_first is None
    tm = min(256, s)
    row = lambda x: x.astype(F32).reshape(1, -1)
    mu_low = jnp.concatenate([mu[3 * width:], v_mu if not first else jnp.zeros((RW_V_RANK,), F32),
                              jnp.zeros((RW_LOW - RW_W_RANK - RW_A_RANK - RW_V_RANK,), F32)])
    lora = jnp.zeros((RW_LOW, 3 * width), F32)
    lora = lora.at[0:RW_W_RANK, 0:width].set(w2)
    lora = lora.at[RW_W_RANK:RW_W_RANK + RW_A_RANK, width:2 * width].set(a2)
    if not first:
        lora = lora.at[RW_W_RANK + RW_A_RANK:RW_W_RANK + RW_A_RANK + RW_V_RANK, 2 * width:].set(v2)
        v0_row = row(v0)
    else:
        v0_row = jnp.zeros((1, width), F32)
    blk3 = 3 * width // V7X_LANES
    tile = lambda w_: pl.BlockSpec((1, tm, w_), lambda bi, i: (bi, i, 0))
    prev = lambda w_, col: pl.BlockSpec((1, 8, w_), lambda bi, i: (bi, jnp.maximum(i * (tm // 8) - 1, 0), col))
    full = lambda x: pl.BlockSpec(x.shape, lambda bi, i: (0,) * x.ndim)
    mu_row, mu_low_row, lora_b = row(mu[:3 * width]), row(mu_low), lora.astype(BF16)
    consts = [mu_row, mu_low_row, lora_b, row(w0), row(a0), v0_row, row(k_k), row(k_a)]
    assert zm_col % blk3 == 0
    in_specs = [pl.BlockSpec((1, tm, 3 * width), lambda bi, i: (bi, i, zm_col // blk3)),
                prev(3 * width, zm_col // blk3), tile(RW_LOW), prev(RW_LOW, 0)]
    args = [zm, zm, low, low]
    if not first:
        in_specs.append(tile(width))
        args.append(v_first)
    in_specs += [full(x) for x in consts]
    args += consts
    outs = pl.pallas_call(
        functools.partial(_rw_prep_kernel, first=first),
        grid=(b, s // tm),
        in_specs=in_specs,
        out_specs=[tile(width)] * 6,
        out_shape=[jax.ShapeDtypeStruct((b, s, width), F32)] * 6,
        compiler_params=_params("parallel", "parallel"),
        name="rwkv_prep",
    )(*args)
    r, lw, k, v, a, kk = outs

    c = RW_CHUNK
    blk = min(RW_BLOCK, s)
    pw = 2 * RW_HEAD
    n_pairs = width // pw
    tril = np.tril(np.ones((c, c), np.float32))
    tri = jnp.asarray(tril, BF16)
    two = np.eye(2, dtype=np.float32)
    bd = jnp.asarray(np.stack([np.kron(two, np.tril(tril, -1)), np.kron(two, tril)]))
    n_chunks = blk // c
    npg = min(RW_PAIRS, n_pairs)
    gw = npg * pw
    assert n_pairs % npg == 0 and (zg_col * V7X_LANES) % gw == 0
    seq = lambda off=0: pl.BlockSpec((1, blk, gw), lambda bi, p, i: (bi, i, off + p))
    par = pl.BlockSpec((1, gw), lambda bi, p, i: (0, p))
    y = pl.pallas_call(
        functools.partial(_rw_chunk_kernel, c=c, n_chunks=n_chunks, n_pairs=npg),
        grid=(b, n_pairs // npg, s // blk),
        in_specs=[seq()] * 6 + [seq(zg_col * V7X_LANES // gw), par, par, par,
                                pl.BlockSpec(tri.shape, lambda bi, p, i: (0, 0)),
                                pl.BlockSpec(bd.shape, lambda bi, p, i: (0, 0, 0))],
        out_specs=seq(),
        out_shape=jax.ShapeDtypeStruct((b, s, width), BF16),
        scratch_shapes=[pltpu.VMEM((npg, pw, pw), F32), pltpu.VMEM((npg, n_chunks, c + pw, pw), BF16),
                        pltpu.VMEM((npg, n_chunks, c + pw, pw), F32), pltpu.VMEM((npg, blk, pw), F32)],
        compiler_params=_params("parallel", "parallel", "arbitrary"),
        name="rwkv_chunk",
    )(r, lw, k, v, a, kk, zg, row(r_k), row(lnx_g), row(lnx_b), tri, bd)
    return y, (v if first else v_first)


def _merge_kernel(ya_ref, yb_ref, yc_ref, yd_ref, g0_ref, g1_ref, g2_ref, g3_ref, wb_ref, o_ref):
    acc = None
    for n, (y_ref, g_ref) in enumerate(zip((ya_ref, yb_ref, yc_ref, yd_ref), (g0_ref, g1_ref, g2_ref, g3_ref))):
        term = _sigmoid(g_ref[...].astype(F32)) * _dot(y_ref[...], wb_ref[n])
        acc = term if acc is None else acc + term
    o_ref[...] = acc.astype(o_ref.dtype)


def _merge(ys, z2d, gate_col, wb):
    m, w = ys[0].shape
    d = wb.shape[-1]
    tm = min(256, m)
    y_spec = pl.BlockSpec((tm, w), lambda i: (i, 0))
    g_spec = lambda n: pl.BlockSpec((tm, d), lambda i: (i, gate_col + n))
    return pl.pallas_call(
        _merge_kernel,
        grid=(m // tm,),
        in_specs=[y_spec] * N_BRANCH + [g_spec(n) for n in range(N_BRANCH)]
        + [pl.BlockSpec(wb.shape, lambda i: (0, 0, 0))],
        out_specs=pl.BlockSpec((tm, d), lambda i: (i, 0)),
        out_shape=jax.ShapeDtypeStruct((m, d), BF16),
        compiler_params=_params("parallel"),
        name="merge",
    )(*ys, z2d, z2d, z2d, z2d, wb)


def _out_ln_kernel(m_ref, w_ref, h_ref, g_ref, b_ref, o_ref, ob_ref, *, alpha):
    x = alpha * h_ref[...] + _dot(m_ref[...], w_ref[...])
    mu = jnp.mean(x, axis=-1, keepdims=True)
    cen = x - mu
    var = jnp.mean(cen * cen, axis=-1, keepdims=True)
    y = cen * lax.rsqrt(var + LN_EPS) * g_ref[...] + b_ref[...]
    o_ref[...] = y
    ob_ref[...] = y.astype(ob_ref.dtype)


def _out_ln(merged, w_out, h, ln_g, ln_b, alpha):
    m, d = h.shape
    tm = min(512, m)
    tile = pl.BlockSpec((tm, d), lambda i: (i, 0))
    vec = pl.BlockSpec((1, d), lambda i: (0, 0))
    return pl.pallas_call(
        functools.partial(_out_ln_kernel, alpha=alpha),
        grid=(m // tm,),
        in_specs=[tile, pl.BlockSpec(w_out.shape, lambda i: (0, 0)), tile, vec, vec],
        out_specs=[tile, tile],
        out_shape=[jax.ShapeDtypeStruct((m, d), F32), jax.ShapeDtypeStruct((m, d), BF16)],
        compiler_params=_params("parallel"),
        name="out_ln",
    )(merged, w_out, h, ln_g.astype(F32).reshape(1, d), ln_b.astype(F32).reshape(1, d))


def kernel(x, w_in, rel_bias, da_lambda, da_subln, hg_lower, hg_norm, rw_mu, rw_w0, rw_w2, rw_a0, rw_a2, rw_v1,
           rw_v_mu, rw_v0, rw_v2, rw_kk, rw_ka, rw_rk, rw_lnx_g, rw_lnx_b, w_branch, w_out, ln_g, ln_b):
    b, s, d = x.shape
    depth = w_in.shape[0]
    width = w_branch.shape[2]
    heads = width // V7X_LANES
    m = b * s
    alpha = (2 * depth) ** 0.25
    gb = width // V7X_LANES

    off_a, off_b, off_c = 0, 4 * width, 8 * width
    rw_mix = 3 * width + RW_W_RANK + RW_A_RANK
    off_cg = off_c + rw_mix
    off_d = off_cg + width
    off_m = off_d + 4 * width
    col_c, col_cg, col_a, col_b, col_d = 0, 3 * gb, 4 * gb, 8 * gb, 12 * gb
    col_m = 16 * gb

    lbs = jnp.cumsum(jax.nn.softmax(hg_lower.astype(F32), axis=0), axis=0)
    lbs = lbs - lbs[0:1]
    bias_tiles = _bias_tiles(rel_bias, min(ATT_TILE, s))

    h = x.reshape(m, d)
    hb = h.astype(BF16)
    v_first = None
    for l in range(depth):
        wl = w_in[l]
        w_main = jnp.concatenate([wl[:, off_c:off_c + 3 * width], wl[:, off_cg:off_cg + width],
                                  wl[:, off_a:off_a + 4 * width], wl[:, off_b:off_b + 4 * width],
                                  wl[:, off_d:off_d + 4 * width], wl[:, off_m:]], axis=1).astype(BF16)
        low_cols = [wl[:, off_c + 3 * width:off_cg]]
        if l > 0:
            low_cols.append(rw_v1[l - 1])
        n_low = sum(c.shape[1] for c in low_cols)
        low_cols.append(jnp.zeros((d, RW_LOW - n_low), F32))
        w_low = jnp.concatenate(low_cols, axis=1).astype(BF16)

        z2d = _matmul(hb, w_main, BF16, min(1024, m), 1024, "in_proj")
        low = _matmul(hb, w_low, F32, min(1024, m), RW_LOW, "in_proj_low").reshape(b, s, RW_LOW)
        z = z2d.reshape(b, s, -1)

        y_a = _diff_attention(z, (col_a, col_a + gb, col_a + 2 * gb, col_a + 3 * gb), bias_tiles, da_lambda[l],
                              da_subln[l], l, heads)
        y_b = _hgrn2(z, (col_b, col_b + gb, col_b + 2 * gb, col_b + 3 * gb), lbs[l], hg_norm[l], heads)
        y_c, v_first = _rwkv7(z, z, low, v_first, rw_mu[l], rw_w0[l], rw_w2[l], rw_a0[l], rw_a2[l],
                              rw_v_mu[l - 1] if l > 0 else None, rw_v0[l - 1] if l > 0 else None,
                              rw_v2[l - 1] if l > 0 else None, rw_kk[l], rw_ka[l], rw_rk[l], rw_lnx_g[l],
                              rw_lnx_b[l], zm_col=col_c, zg_col=col_cg)
        y_d = _stick_breaking(z, (col_d, col_d + gb, col_d + 2 * gb, col_d + 3 * gb), heads, V7X_LANES)

        merged = _merge([y.reshape(m, width) for y in (y_a, y_b, y_c, y_d)], z2d, col_m * V7X_LANES // d,
                        w_branch[l].astype(BF16))
        h, hb = _out_ln(merged, w_out[l].astype(BF16), h, ln_g[l], ln_b[l], alpha)
    return h.reshape(b, s, d)
```

```python
import functools
import math

import numpy as np
import jax
import jax.numpy as jnp
from jax import lax
from jax.experimental import pallas as pl
from jax.experimental.pallas import tpu as pltpu

F32 = jnp.float32
BF16 = jnp.bfloat16

V7X_LANES = 128
V7X_VMEM_LIMIT_BYTES = 56 * 1024 * 1024

N_BRANCH = 4
DA_QK = 64
REL_BUCKETS = 32
REL_MAX_DIST = 128
HG_CHUNK = 64
RW_HEAD = 64
RW_W_RANK = 64
RW_A_RANK = 64
RW_V_RANK = 32
RW_LN_EPS = 64e-5
RW_CHUNK = 64
LN_EPS = 1e-5
RMS_EPS = 1e-6
NEG_BIG = -1e30

ATT_TILE = 512
ATT_STRIP = 64
HG_GROUP = 16
RW_BLOCK = 512
RW_PAIRS = 4


def _params(*sem):
    return pltpu.CompilerParams(dimension_semantics=sem, vmem_limit_bytes=V7X_VMEM_LIMIT_BYTES)


def _dot(a, b):
    return jnp.dot(a, b, preferred_element_type=F32)


def _dot_nt(a, b):
    return lax.dot_general(a, b, (((1,), (1,)), ((), ())), preferred_element_type=F32)


def _dot_tn(a, b):
    return lax.dot_general(a, b, (((0,), (0,)), ((), ())), preferred_element_type=F32)


def _sigmoid(x):
    return 1.0 / (1.0 + jnp.exp(-x))


def _silu(x):
    return x * _sigmoid(x)


def _split_bf16(x):
    hi = x.astype(BF16)
    lo = (x - hi.astype(F32)).astype(BF16)
    return hi, lo


def _lock_step(generators):
    live = list(generators)
    while live:
        for gen in list(live):
            try:
                next(gen)
            except StopIteration:
                live.remove(gen)


def _matmul_kernel(x_ref, w_ref, o_ref):
    o_ref[...] = _dot(x_ref[...], w_ref[...]).astype(o_ref.dtype)


def _matmul(x, w, out_dtype, tm, tn, name):
    m, k = x.shape
    n = w.shape[1]
    return pl.pallas_call(
        _matmul_kernel,
        grid=(m // tm, n // tn),
        in_specs=[pl.BlockSpec((tm, k), lambda i, j: (i, 0)),
                  pl.BlockSpec((k, tn), lambda i, j: (0, j))],
        out_specs=pl.BlockSpec((tm, tn), lambda i, j: (i, j)),
        out_shape=jax.ShapeDtypeStruct((m, n), out_dtype),
        compiler_params=_params("parallel", "arbitrary"),
        name=name,
    )(x, w)


def _t5_bucket_np(dist):
    max_exact = REL_BUCKETS // 2
    n = np.maximum(dist, 0)
    nf = np.maximum(n, 1).astype(np.float32)
    large = max_exact + (np.log(nf / np.float32(max_exact)) / np.float32(math.log(REL_MAX_DIST / max_exact))
                         * np.float32(REL_BUCKETS - max_exact)).astype(np.int32)
    large = np.minimum(large, REL_BUCKETS - 1)
    return np.where(n < max_exact, n, large)


def _bias_tiles(rel_bias, t):
    assert t + 1 >= REL_MAX_DIST
    rows = np.arange(t)[:, None]
    cols = np.arange(t)[None, :]
    dist = np.stack([behind * t + rows - cols for behind in range(3)])
    bucket = jnp.asarray(_t5_bucket_np(dist).astype(np.int32))
    table = rel_bias.astype(F32).T * math.log2(math.e)
    n_heads = table.shape[0]
    tiles = jnp.zeros((n_heads,) + dist.shape, F32)
    for bkt in range(REL_BUCKETS):
        tiles = jnp.where(bucket[None] == bkt, table[:, bkt][:, None, None, None], tiles)
    tiles = jnp.where(jnp.asarray(dist >= 0)[None], tiles, NEG_BIG)
    return jnp.concatenate([tiles, jnp.full((n_heads, 1, t, t), NEG_BIG, F32)], axis=1)


def _causal_walk(nq, depth, n_kinds):
    steps = [(i, i - jj, min(jj, n_kinds - 2), int(jj == 0), int(jj == i)) for i in range(nq) for jj in range(i + 1)]
    pad = (steps[-1][0], steps[-1][1], n_kinds - 1, 0, 0)
    return np.asarray(steps + [pad] * (depth - 1), np.int32).T


def _da_kernel(tab_ref, lam_ref, q_ref, k_ref, v_ref, g_ref, bias_ref, subln_ref, o_ref,
               qm_ref, s_ref, p_ref, alpha_ref, m_ref, acc_ref, *, lam_init, t, n_steps, strip):
    w = 2 * DA_QK
    lane = lax.broadcasted_iota(jnp.int32, q_ref.shape[1:], 1)
    q = (q_ref[0].astype(F32) * (DA_QK ** -0.5 * math.log2(math.e))).astype(BF16)
    zero = jnp.zeros_like(q)
    qm_ref[0] = jnp.where(lane < DA_QK, q, zero)
    qm_ref[1] = jnp.where(lane >= DA_QK, q, zero)
    @pl.when(jnp.logical_and(pl.program_id(0) == 0, pl.program_id(1) == 0))
    def _():
        s_ref[...] = jnp.zeros(s_ref.shape, F32)
        p_ref[...] = jnp.zeros(p_ref.shape, BF16)
        alpha_ref[...] = jnp.ones(alpha_ref.shape, F32)
        m_ref[...] = jnp.zeros(m_ref.shape, F32)
        acc_ref[...] = jnp.zeros(acc_ref.shape, F32)

    ones = jnp.ones((t, w), BF16)
    lam = lam_ref[...]
    lam_full = (jnp.exp(jnp.sum(lam[0:1] * lam[1:2], axis=-1, keepdims=True))
                - jnp.exp(jnp.sum(lam[2:3] * lam[3:4], axis=-1, keepdims=True)) + lam_init)
    dead = bias_ref.shape[1] - 1

    def tile_rows(idx):
        return pl.ds(pl.multiple_of(idx * t, t), t)

    def stages(it, half):
        c = jnp.maximum(it - 2, 0)
        slot = tab_ref[0, c] % 2
        v_ext = jnp.concatenate([v_ref[0, tile_rows(tab_ref[1, c]), :], ones], axis=1)
        for m_i in range(2):
            acc_ref[slot, m_i] = (jnp.tile(alpha_ref[half, m_i], (1, 2)) * acc_ref[slot, m_i]
                                  + _dot(p_ref[half, m_i], v_ext))

        kj = k_ref[0, tile_rows(tab_ref[1, it]), :]
        q_rows = tile_rows(tab_ref[0, it])
        for m_i in range(2):
            s_ref[half, m_i] = _dot_nt(qm_ref[m_i, q_rows, :], kj)

        b = jnp.maximum(it - 1, 0)
        kind = jnp.where(it >= 1, tab_ref[2, b], dead)
        first = jnp.logical_and(it >= 1, tab_ref[3, b] == 1)
        for r0 in range(0, t, strip):
            rows = pl.ds(r0, strip)
            bias = bias_ref[0, kind, rows, :]
            for m_i in range(2):
                s = s_ref[1 - half, m_i, rows, :] + bias
                m_prev = jnp.where(first, NEG_BIG, m_ref[m_i, rows, :])
                m_new = jnp.maximum(m_prev, jnp.max(s, axis=-1, keepdims=True))
                p_ref[1 - half, m_i, rows, :] = jnp.exp2(s - jnp.tile(m_new, (1, t // w))).astype(BF16)
                alpha_ref[1 - half, m_i, rows, :] = jnp.exp2(m_prev - m_new)
                m_ref[m_i, rows, :] = m_new

    def finish(it):
        c = jnp.maximum(it - 2, 0)

        @pl.when(jnp.logical_and(it >= 2, tab_ref[4, c] == 1))
        def _():
            rows = tile_rows(tab_ref[0, c])
            slot = tab_ref[0, c] % 2
            o = (acc_ref[slot, 0, :, 0:w] / acc_ref[slot, 0, :, w:2 * w]
                 - lam_full * (acc_ref[slot, 1, :, 0:w] / acc_ref[slot, 1, :, w:2 * w]))
            o = o * lax.rsqrt(jnp.mean(o * o, axis=-1, keepdims=True) + RMS_EPS) * subln_ref[...] * (1.0 - lam_init)
            o_ref[0, rows, :] = (o * _silu(g_ref[0, rows, :].astype(F32))).astype(o_ref.dtype)

    def body(i2, carry):
        for half in range(2):
            stages(2 * i2 + half, half)
        for half in range(2):
            finish(2 * i2 + half)
        return carry

    lax.fori_loop(0, (n_steps + 3) // 2, body, 0)


def _diff_attention(z, cols, bias_tiles, lam, subln, layer, n_heads):
    b, s, _ = z.shape
    t = bias_tiles.shape[-1]
    lam_init = 0.8 - 0.6 * math.exp(-0.3 * layer)
    cq, ck, cv, cg = cols
    w = 2 * DA_QK
    tab = _causal_walk(s // t, 4, 4)
    n_steps = tab.shape[1] - 3
    seq = lambda off: pl.BlockSpec((1, s, w), lambda bi, h, tab_: (bi, 0, off + h))
    return pl.pallas_call(
        functools.partial(_da_kernel, lam_init=lam_init, t=t, n_steps=n_steps, strip=min(ATT_STRIP, t)),
        grid_spec=pltpu.PrefetchScalarGridSpec(
            num_scalar_prefetch=1,
            grid=(b, n_heads),
            in_specs=[
                pl.BlockSpec((4, DA_QK), lambda bi, h, tab_: (0, 0)),
                seq(cq), seq(ck), seq(cv), seq(cg),
                pl.BlockSpec((1, 4, t, t), lambda bi, h, tab_: (h, 0, 0, 0)),
                pl.BlockSpec((1, w), lambda bi, h, tab_: (0, 0)),
            ],
            out_specs=pl.BlockSpec((1, s, w), lambda bi, h, tab_: (bi, 0, h)),
            scratch_shapes=[pltpu.VMEM((2, s, w), BF16), pltpu.VMEM((2, 2, t, t), F32),
                            pltpu.VMEM((2, 2, t, t), BF16), pltpu.VMEM((2, 2, t, w), F32),
                            pltpu.VMEM((2, t, w), F32), pltpu.VMEM((2, 2, t, 2 * w), F32)],
        ),
        out_shape=jax.ShapeDtypeStruct((b, s, n_heads * w), BF16),
        compiler_params=_params("arbitrary", "arbitrary"),
        name="diff_attention",
    )(jnp.asarray(tab), lam.astype(F32), z, z, z, z, bias_tiles, subln.astype(F32).reshape(1, w))


SB_SUB = 256


def _sb_kernel(tab_ref, q_ref, k_ref, v_ref, g_ref, tri_ref, neg_ref, o_ref,
               qs_ref, s_ref, z_ref, lk_ref, later_ref, p_ref, acc_ref, *, t, d, n_steps):
    n_sub = t // SB_SUB
    qs_ref[...] = (q_ref[0].astype(F32) * (d ** -0.5 * math.log2(math.e))).astype(BF16)
    @pl.when(jnp.logical_and(pl.program_id(0) == 0, pl.program_id(1) == 0))
    def _():
        s_ref[...] = jnp.zeros(s_ref.shape, F32)
        z_ref[...] = jnp.full(z_ref.shape, NEG_BIG, F32)
        lk_ref[...] = jnp.zeros(lk_ref.shape, BF16)
        later_ref[...] = jnp.zeros(later_ref.shape, F32)
        p_ref[...] = jnp.zeros(p_ref.shape, BF16)
        acc_ref[...] = jnp.zeros(acc_ref.shape, F32)

    tri = tri_ref[...]
    dead = neg_ref.shape[0] - 1

    def tile_rows(idx):
        return pl.ds(pl.multiple_of(idx * t, t), t)

    def stages(it, half):
        c3 = jnp.maximum(it - 2, 0)
        later = later_ref[...] * jnp.where(jnp.logical_and(it >= 2, tab_ref[3, c3] == 1), 0.0, 1.0)
        for c in reversed(range(n_sub)):
            cols = slice(c * SB_SUB, (c + 1) * SB_SUB)
            sums = _dot(lk_ref[half, :, cols], tri)
            p_ref[half, :, cols] = jnp.exp2(z_ref[half, :, cols] + sums
                                            + jnp.tile(later, (1, SB_SUB // d))).astype(BF16)
            later = later + sums[:, 0:1]
        later_ref[...] = later

        s_ref[half] = _dot_nt(qs_ref[tile_rows(tab_ref[0, it]), :], k_ref[0, tile_rows(tab_ref[1, it]), :])

        c4 = jnp.maximum(it - 3, 0)
        slot = tab_ref[0, c4] % 2
        acc_ref[slot] += _dot(p_ref[1 - half], v_ref[0, tile_rows(tab_ref[1, c4]), :])

        b2 = jnp.maximum(it - 1, 0)
        zs = s_ref[1 - half] + neg_ref[jnp.where(it >= 1, tab_ref[2, b2], dead)]
        neg_z = -zs
        soft = jnp.log(1.0 + jnp.exp2(jnp.minimum(zs, neg_z))) * math.log2(math.e)
        z_ref[1 - half] = zs
        lk_ref[1 - half] = (jnp.minimum(neg_z, 0.0) - soft).astype(BF16)

    def finish(it):
        c4 = jnp.maximum(it - 3, 0)

        @pl.when(jnp.logical_and(it >= 3, tab_ref[4, c4] == 1))
        def _():
            rows = tile_rows(tab_ref[0, c4])
            slot = tab_ref[0, c4] % 2
            o_ref[0, rows, :] = (acc_ref[slot] * _silu(g_ref[0, rows, :].astype(F32))).astype(o_ref.dtype)
            acc_ref[slot] = jnp.zeros(acc_ref.shape[1:], F32)

    def body(i2, carry):
        for half in range(2):
            stages(2 * i2 + half, half)
        for half in range(2):
            finish(2 * i2 + half)
        return carry

    lax.fori_loop(0, (n_steps + 4) // 2, body, 0)


def _stick_breaking(z, cols, n_heads, d):
    b, s, _ = z.shape
    t = min(ATT_TILE, s)
    cq, ck, cv, cg = cols
    tab = _causal_walk(s // t, 5, 3)
    n_steps = tab.shape[1] - 4
    tri = jnp.asarray(np.tril(np.ones((SB_SUB, SB_SUB), np.float32)), BF16)
    strict = np.tril(np.ones((t, t), np.float32), -1)
    neg = jnp.asarray(np.stack([(strict - 1.0), np.zeros_like(strict), -np.ones_like(strict)]) * -NEG_BIG)
    seq = lambda off: pl.BlockSpec((1, s, d), lambda bi, h, tab_: (bi, 0, off + h))
    const = lambda x: pl.BlockSpec(x.shape, lambda bi, h, tab_: (0,) * x.ndim)
    return pl.pallas_call(
        functools.partial(_sb_kernel, t=t, d=d, n_steps=n_steps),
        grid_spec=pltpu.PrefetchScalarGridSpec(
            num_scalar_prefetch=1,
            grid=(b, n_heads),
            in_specs=[seq(cq), seq(ck), seq(cv), seq(cg), const(tri), const(neg)],
            out_specs=pl.BlockSpec((1, s, d), lambda bi, h, tab_: (bi, 0, h)),
            scratch_shapes=[pltpu.VMEM((s, d), BF16), pltpu.VMEM((2, t, t), F32), pltpu.VMEM((2, t, t), F32),
                            pltpu.VMEM((2, t, t), BF16), pltpu.VMEM((t, d), F32), pltpu.VMEM((2, t, t), BF16),
                            pltpu.VMEM((2, t, d), F32)],
        ),
        out_shape=jax.ShapeDtypeStruct((b, s, n_heads * d), BF16),
        compiler_params=_params("arbitrary", "arbitrary"),
        name="stick_breaking",
    )(jnp.asarray(tab), z, z, z, z, tri, neg)


def _hg_level_sizes(c):
    sizes = []
    m = c // 2
    while m >= 1:
        sizes.append(m)
        m //= 2
    return sizes


def _hg_sum_matrix(c):
    r = np.arange(c)[:, None]
    s = np.arange(c)[None, :]
    blocks = []
    for m in _hg_level_sizes(c):
        start = (r // m) * m
        upper = (r // m) % 2 == 1
        lower_part = (s > r) & (s <= start + m - 1)
        blocks.append(np.where(upper, (s >= start) & (s <= r), lower_part))
    blocks.append(s <= r)
    blocks.append(s > r)
    return np.concatenate(blocks, 0).astype(np.float32)


def _hg_level_masks(c):
    r = np.arange(c)[:, None]
    s = np.arange(c)[None, :]
    masks = [((r // m) % 2 == 1) & ((s // m) == (r // m) - 1) for m in _hg_level_sizes(c)]
    masks.append(r == s)
    return np.stack(masks, 0).astype(np.float32)


def _hg_kernel(q_ref, f_ref, i_ref, g_ref, lb_ref, nrm_ref, sums_ref, masks_ref, o_ref, *, c, n_chunks, group):
    n_levels = len(_hg_level_sizes(c))
    lb = lb_ref[...]
    log_lb = jnp.log(lb)
    log_1m_lb = jnp.log1p(-lb)
    sums = sums_ref[...]
    nrm = nrm_ref[...]
    dk = q_ref.shape[-1]

    def chunk_pair(first, slot, out):
        rows = pl.ds(pl.multiple_of(first * c, 2 * c), 2 * c)
        q2 = q_ref[0, rows, :].astype(F32)
        zf = f_ref[0, rows, :].astype(F32)
        inp2 = i_ref[0, rows, :]
        e = jnp.exp(-jnp.abs(zf))
        log_sig = jnp.minimum(zf, 0.0) - jnp.log1p(e)
        arg = log_1m_lb + log_sig
        log_f = jnp.maximum(log_lb, arg) + jnp.log1p(jnp.exp(-jnp.abs(log_lb - arg)))
        k2 = (1.0 - lb) * jnp.where(zf >= 0.0, e, 1.0) / (1.0 + e)
        log2_f = log_f * math.log2(math.e)
        lf = jnp.concatenate([log2_f[:c], log2_f[c:]], axis=1).astype(BF16)
        d_cat = jnp.exp2(_dot(sums, lf))
        yield
        halves = []
        for u in range(2):
            q = q2[u * c:(u + 1) * c]
            k_in = k2[u * c:(u + 1) * c]
            d_all = d_cat[:, u * dk:(u + 1) * dk]
            prods = [_dot_nt(q.astype(BF16), k_in.astype(BF16))]
            for lv in range(n_levels):
                x = d_all[lv * c:(lv + 1) * c]
                prods.append(_dot_nt((q * x).astype(BF16), (k_in * x).astype(BF16)))
            halves.append((q, k_in, d_all, prods))
        yield
        for u, (q, k_in, d_all, prods) in enumerate(halves):
            scores = masks_ref[n_levels] * prods[0]
            for lv in range(n_levels):
                scores = scores + masks_ref[lv] * prods[lv + 1]
            eb = d_all[n_levels * c:(n_levels + 1) * c]
            rev = d_all[(n_levels + 1) * c:(n_levels + 2) * c]
            inp = inp2[u * c:(u + 1) * c]
            out[slot + u] = ((q * eb).astype(BF16), _dot(scores.astype(BF16), inp), eb[c - 1:c],
                             _dot_tn(inp, (k_in * rev).astype(BF16)))
        yield

    def body(gi, state_t):
        base = gi * group
        pre = {}
        _lock_step([chunk_pair(base + u, u, pre) for u in range(0, group, 2)])
        states = []
        for u in range(group):
            _, _, decay, update = pre[u]
            states.append(state_t.astype(BF16))
            state_t = state_t * decay + update
        inter = [_dot_nt(pre[u][0], states[u]) for u in range(group)]
        for u in range(group):
            rows = pl.ds(pl.multiple_of((base + u) * c, c), c)
            o = inter[u] + pre[u][1]
            y = o * lax.rsqrt(jnp.mean(o * o, axis=-1, keepdims=True) + RMS_EPS) * nrm
            o_ref[0, rows, :] = (y * _silu(g_ref[0, rows, :].astype(F32))).astype(o_ref.dtype)
        return state_t

    lax.fori_loop(0, n_chunks // group, body, jnp.zeros((i_ref.shape[-1], dk), F32))


def _hgrn2(z, cols, lb, nrm, n_heads):
    b, s, _ = z.shape
    c = HG_CHUNK
    d = V7X_LANES
    cq, cf, ci, cg = cols
    sums = jnp.asarray(_hg_sum_matrix(c), BF16)
    masks = jnp.asarray(_hg_level_masks(c), F32)
    seq = lambda off: pl.BlockSpec((1, s, d), lambda bi, h: (bi, 0, off + h))
    return pl.pallas_call(
        functools.partial(_hg_kernel, c=c, n_chunks=s // c, group=min(HG_GROUP, s // c)),
        grid=(b, n_heads),
        in_specs=[seq(cq), seq(cf), seq(ci), seq(cg),
                  pl.BlockSpec((1, d), lambda bi, h: (0, h)),
                  pl.BlockSpec((1, d), lambda bi, h: (0, 0)),
                  pl.BlockSpec(sums.shape, lambda bi, h: (0, 0)),
                  pl.BlockSpec(masks.shape, lambda bi, h: (0, 0, 0))],
        out_specs=pl.BlockSpec((1, s, d), lambda bi, h: (bi, 0, h)),
        out_shape=jax.ShapeDtypeStruct((b, s, n_heads * d), BF16),
        compiler_params=_params("parallel", "parallel"),
        name="hgrn2",
    )(z, z, z, z, lb.astype(F32).reshape(1, n_heads * d), nrm.astype(F32).reshape(1, d), sums, masks)


RW_LOW = 256


def _rw_prep_kernel(*refs, first):
    if first:
        (zm_ref, zm_prev_ref, low_ref, low_prev_ref, mu_ref, mu_low_ref, lora_ref, w0_ref, a0_ref, v0_ref,
         kk_ref, ka_ref, r_out, lw_out, k_out, v_out, a_out, kk_out) = refs
        vf_ref = None
    else:
        (zm_ref, zm_prev_ref, low_ref, low_prev_ref, vf_ref, mu_ref, mu_low_ref, lora_ref, w0_ref, a0_ref,
         v0_ref, kk_ref, ka_ref, r_out, lw_out, k_out, v_out, a_out, kk_out) = refs
    i = pl.program_id(1)
    width = r_out.shape[-1]

    def lerp_shift(x, prev_block, mu):
        row = lax.broadcasted_iota(jnp.int32, x.shape, 0)
        prev = jnp.where(i > 0, prev_block[7:8, :], 0.0)
        shifted = jnp.where(row == 0, prev, pltpu.roll(x, 1, 0))
        return x + (shifted - x) * mu

    zm = lerp_shift(zm_ref[0].astype(F32), zm_prev_ref[0].astype(F32), mu_ref[...])
    low = lerp_shift(low_ref[0], low_prev_ref[0], mu_low_ref[...])
    lane = lax.broadcasted_iota(jnp.int32, low.shape, 1)
    low = jnp.where(lane < RW_W_RANK, jnp.tanh(low), low)
    lora = _dot(low.astype(BF16), lora_ref[...])
    r = zm[:, 0:width]
    k = zm[:, width:2 * width]
    v = zm[:, 2 * width:3 * width]
    x = -(w0_ref[...] + lora[:, 0:width])
    w_log = -(jnp.maximum(x, 0.0) + jnp.log1p(jnp.exp(-jnp.abs(x)))) - 0.5
    a = _sigmoid(a0_ref[...] + lora[:, width:2 * width])
    if not first:
        v = v + (vf_ref[0] - v) * _sigmoid(v0_ref[...] + lora[:, 2 * width:3 * width])
    r_out[0] = r.astype(r_out.dtype)
    lw_out[0] = -jnp.exp(w_log)
    k_out[0] = (k * (1.0 + (a - 1.0) * ka_ref[...])).astype(k_out.dtype)
    v_out[0] = v
    a_out[0] = a.astype(a_out.dtype)
    kk_out[0] = (k * kk_ref[...]).astype(kk_out.dtype)


def _rw_chunk_kernel(r_ref, lw_ref, k_ref, v_ref, a_ref, kk_ref, g_ref, rk_ref, lng_ref, lnb_ref, tri_ref, bd_ref,
                     o_ref, state_ref, gp_ref, hq_ref, oraw_ref, *, c, n_chunks, n_pairs):
    hd = RW_HEAD

    @pl.when(pl.program_id(2) == 0)
    def _():
        state_ref[...] = jnp.zeros_like(state_ref)

    pw = 2 * hd
    lane = lax.broadcasted_iota(jnp.int32, (1, pw), 1)
    head_masks = ((lane < hd).astype(F32), (lane >= hd).astype(F32))
    in_h0 = lane < hd
    tri_inc_b = tri_ref[...]
    strict2 = bd_ref[0]
    incl2 = bd_ref[1]
    rr = lax.broadcasted_iota(jnp.int32, (pw, pw), 0)
    cc = lax.broadcasted_iota(jnp.int32, (pw, pw), 1)
    eye_p = rr == cc
    eye_f = eye_p.astype(F32)

    def head_sum(x):
        s0 = jnp.sum(x * head_masks[0], axis=-1, keepdims=True)
        s1 = jnp.sum(x * head_masks[1], axis=-1, keepdims=True)
        return jnp.where(in_h0, s0, s1)

    def stack_heads(x):
        return jnp.concatenate([x * head_masks[0], x * head_masks[1]], axis=0)

    def both(x):
        return x[:, :pw] + x[:, pw:]

    def transition(pi, ci):
        rows = pl.ds(pl.multiple_of(ci * c, c), c)
        lanes = slice(pi * pw, (pi + 1) * pw)
        r = r_ref[0, rows, lanes].astype(F32)
        lw = lw_ref[0, rows, lanes]
        k = k_ref[0, rows, lanes].astype(F32)
        v = v_ref[0, rows, lanes]
        a = a_ref[0, rows, lanes].astype(F32)
        kk = kk_ref[0, rows, lanes].astype(F32)
        lw = lw * math.log2(math.e)
        hi, lo = _split_bf16(lw)
        g = both(_dot(tri_inc_b, jnp.concatenate([hi, lo], axis=1)))
        yield
        g_last = g[c - 1:c]
        e_in = jnp.exp2(g)
        e_ex = jnp.exp2(g - lw)
        e_neg = jnp.exp2(-g)
        e_rev = jnp.exp2(g_last - g)
        kkn = kk / jnp.maximum(jnp.sqrt(head_sum(kk * kk)), 1e-12)
        b_vec = kkn * a
        a2 = stack_heads(-kkn * e_ex).astype(BF16)
        r2 = stack_heads(r * e_in)
        v2 = stack_heads(v).astype(BF16)
        x = jnp.concatenate([a2, r2.astype(BF16)], axis=0)
        y = jnp.concatenate([b_vec * e_neg, b_vec * e_neg, k * e_neg, k * e_neg], axis=0).astype(BF16)
        sc = _dot_nt(x, y)
        yield
        l_ab = sc[:pw, :pw] * strict2
        l_ak = (sc[:pw, pw:] * strict2).astype(BF16)
        m_rb = (sc[pw:, :pw] * incl2).astype(BF16)
        m_rk = (sc[pw:, pw:] * incl2).astype(BF16)
        inv = eye_f + l_ab
        lb = l_ab.astype(BF16)
        power = _dot(lb, lb)
        lv = _dot(l_ak, v2).astype(BF16)
        yield
        for _ in range(int(math.log2(c)) - 2):
            pb = power.astype(BF16)
            res = _dot(jnp.concatenate([inv.astype(BF16), pb], axis=0), pb)
            yield
            inv = inv + res[:pw]
            power = res[pw:]
        inv_b = (inv + _dot(inv.astype(BF16), power.astype(BF16))).astype(BF16)
        yield
        tw = _dot(inv_b, jnp.concatenate([a2, lv], axis=1)).astype(BF16)
        yield
        gh = _dot(m_rb, tw)
        g2 = r2 + gh[:, :pw]
        h2 = gh[:, pw:] + _dot(m_rk, v2)
        pq = _dot_tn(stack_heads(b_vec * e_rev).astype(BF16), tw)
        yield
        p_mat = jnp.where(eye_p, jnp.exp2(g_last), 0.0) + pq[:, :pw]
        q_mat = pq[:, pw:] + _dot_tn(stack_heads(k * e_rev).astype(BF16), v2)
        gp_ref[pi, ci, 0:c, :] = (g2[:c] + g2[c:]).astype(BF16)
        gp_ref[pi, ci, c:c + pw, :] = p_mat.astype(BF16)
        hq_ref[pi, ci, 0:c, :] = h2[:c] + h2[c:]
        hq_ref[pi, ci, c:c + pw, :] = q_mat

    def advance(pi):
        for ci in range(n_chunks):
            s_hi, s_lo = _split_bf16(state_ref[pi])
            res = both(_dot(gp_ref[pi, ci], jnp.concatenate([s_hi, s_lo], axis=1))) + hq_ref[pi, ci]
            oraw_ref[pi, ci * c:(ci + 1) * c, :] = res[:c]
            state_ref[pi] = res[c:]
            yield

    for pi in range(n_pairs + 1):
        work = [transition(pi, ci) for ci in range(n_chunks)] if pi < n_pairs else []
        if pi > 0:
            work.append(advance(pi - 1))
        _lock_step(work)

    for pi in range(n_pairs):
        lanes = slice(pi * pw, (pi + 1) * pw)
        o = oraw_ref[pi]
        mean = head_sum(o) * (1.0 / hd)
        cen = o - mean
        var = head_sum(cen * cen) * (1.0 / hd)
        o = cen * lax.rsqrt(var + RW_LN_EPS) * lng_ref[:, lanes] + lnb_ref[:, lanes]
        o = o + head_sum(r_ref[0, :, lanes].astype(F32) * k_ref[0, :, lanes].astype(F32)
                         * rk_ref[:, lanes]) * v_ref[0, :, lanes]
        o_ref[0, :, lanes] = (o * _silu(g_ref[0, :, lanes].astype(F32))).astype(o_ref.dtype)


def _rwkv7(zm, zg, low, v_first, mu, w0, w2, a0, a2, v_mu, v0, v2, k_k, k_a, r_k, lnx_g, lnx_b,
           zm_col=0, zg_col=0):
    b, s, _ = zm.shape
    width = w0.shape[0]
    first = v_first is None
    tm = min(256, s)
    row = lambda x: x.astype(F32).reshape(1, -1)
    mu_low = jnp.concatenate([mu[3 * width:], v_mu if not first else jnp.zeros((RW_V_RANK,), F32),
                              jnp.zeros((RW_LOW - RW_W_RANK - RW_A_RANK - RW_V_RANK,), F32)])
    lora = jnp.zeros((RW_LOW, 3 * width), F32)
    lora = lora.at[0:RW_W_RANK, 0:width].set(w2)
    lora = lora.at[RW_W_RANK:RW_W_RANK + RW_A_RANK, width:2 * width].set(a2)
    if not first:
        lora = lora.at[RW_W_RANK + RW_A_RANK:RW_W_RANK + RW_A_RANK + RW_V_RANK, 2 * width:].set(v2)
        v0_row = row(v0)
    else:
        v0_row = jnp.zeros((1, width), F32)
    blk3 = 3 * width // V7X_LANES
    tile = lambda w_: pl.BlockSpec((1, tm, w_), lambda bi, i: (bi, i, 0))
    prev = lambda w_, col: pl.BlockSpec((1, 8, w_), lambda bi, i: (bi, jnp.maximum(i * (tm // 8) - 1, 0), col))
    full = lambda x: pl.BlockSpec(x.shape, lambda bi, i: (0,) * x.ndim)
    mu_row, mu_low_row, lora_b = row(mu[:3 * width]), row(mu_low), lora.astype(BF16)
    consts = [mu_row, mu_low_row, lora_b, row(w0), row(a0), v0_row, row(k_k), row(k_a)]
    assert zm_col % blk3 == 0
    in_specs = [pl.BlockSpec((1, tm, 3 * width), lambda bi, i: (bi, i, zm_col // blk3)),
                prev(3 * width, zm_col // blk3), tile(RW_LOW), prev(RW_LOW, 0)]
    args = [zm, zm, low, low]
    if not first:
        in_specs.append(tile(width))
        args.append(v_first)
    in_specs += [full(x) for x in consts]
    args += consts
    outs = pl.pallas_call(
        functools.partial(_rw_prep_kernel, first=first),
        grid=(b, s // tm),
        in_specs=in_specs,
        out_specs=[tile(width)] * 6,
        out_shape=[jax.ShapeDtypeStruct((b, s, width), dt) for dt in (BF16, F32, BF16, F32, BF16, BF16)],
        compiler_params=_params("parallel", "parallel"),
        name="rwkv_prep",
    )(*args)
    r, lw, k, v, a, kk = outs

    c = RW_CHUNK
    blk = min(RW_BLOCK, s)
    pw = 2 * RW_HEAD
    n_pairs = width // pw
    tril = np.tril(np.ones((c, c), np.float32))
    tri = jnp.asarray(tril, BF16)
    two = np.eye(2, dtype=np.float32)
    bd = jnp.asarray(np.stack([np.kron(two, np.tril(tril, -1)), np.kron(two, tril)]))
    n_chunks = blk // c
    npg = min(RW_PAIRS, n_pairs)
    gw = npg * pw
    assert n_pairs % npg == 0 and (zg_col * V7X_LANES) % gw == 0
    seq = lambda off=0: pl.BlockSpec((1, blk, gw), lambda bi, p, i: (bi, i, off + p))
    par = pl.BlockSpec((1, gw), lambda bi, p, i: (0, p))
    y = pl.pallas_call(
        functools.partial(_rw_chunk_kernel, c=c, n_chunks=n_chunks, n_pairs=npg),
        grid=(b, n_pairs // npg, s // blk),
        in_specs=[seq()] * 6 + [seq(zg_col * V7X_LANES // gw), par, par, par,
                                pl.BlockSpec(tri.shape, lambda bi, p, i: (0, 0)),
                                pl.BlockSpec(bd.shape, lambda bi, p, i: (0, 0, 0))],
        out_specs=seq(),
        out_shape=jax.ShapeDtypeStruct((b, s, width), BF16),
        scratch_shapes=[pltpu.VMEM((npg, pw, pw), F32), pltpu.VMEM((npg, n_chunks, c + pw, pw), BF16),
                        pltpu.VMEM((npg, n_chunks, c + pw, pw), F32), pltpu.VMEM((npg, blk, pw), F32)],
        compiler_params=_params("parallel", "parallel", "arbitrary"),
        name="rwkv_chunk",
    )(r, lw, k, v, a, kk, zg, row(r_k), row(lnx_g), row(lnx_b), tri, bd)
    return y, (v if first else v_first)


def _merge_kernel(ya_ref, yb_ref, yc_ref, yd_ref, g0_ref, g1_ref, g2_ref, g3_ref, wb_ref, o_ref):
    acc = None
    for n, (y_ref, g_ref) in enumerate(zip((ya_ref, yb_ref, yc_ref, yd_ref), (g0_ref, g1_ref, g2_ref, g3_ref))):
        term = _sigmoid(g_ref[...].astype(F32)) * _dot(y_ref[...], wb_ref[n])
        acc = term if acc is None else acc + term
    o_ref[...] = acc.astype(o_ref.dtype)


def _merge(ys, z2d, gate_col, wb):
    m, w = ys[0].shape
    d = wb.shape[-1]
    tm = min(256, m)
    y_spec = pl.BlockSpec((tm, w), lambda i: (i, 0))
    g_spec = lambda n: pl.BlockSpec((tm, d), lambda i: (i, gate_col + n))
    return pl.pallas_call(
        _merge_kernel,
        grid=(m // tm,),
        in_specs=[y_spec] * N_BRANCH + [g_spec(n) for n in range(N_BRANCH)]
        + [pl.BlockSpec(wb.shape, lambda i: (0, 0, 0))],
        out_specs=pl.BlockSpec((tm, d), lambda i: (i, 0)),
        out_shape=jax.ShapeDtypeStruct((m, d), BF16),
        compiler_params=_params("parallel"),
        name="merge",
    )(*ys, z2d, z2d, z2d, z2d, wb)


def _out_ln_kernel(m_ref, w_ref, h_ref, g_ref, b_ref, o_ref, ob_ref, *, alpha):
    x = alpha * h_ref[...] + _dot(m_ref[...], w_ref[...])
    mu = jnp.mean(x, axis=-1, keepdims=True)
    cen = x - mu
    var = jnp.mean(cen * cen, axis=-1, keepdims=True)
    y = cen * lax.rsqrt(var + LN_EPS) * g_ref[...] + b_ref[...]
    o_ref[...] = y
    ob_ref[...] = y.astype(ob_ref.dtype)


def _out_ln(merged, w_out, h, ln_g, ln_b, alpha):
    m, d = h.shape
    tm = min(512, m)
    tile = pl.BlockSpec((tm, d), lambda i: (i, 0))
    vec = pl.BlockSpec((1, d), lambda i: (0, 0))
    return pl.pallas_call(
        functools.partial(_out_ln_kernel, alpha=alpha),
        grid=(m // tm,),
        in_specs=[tile, pl.BlockSpec(w_out.shape, lambda i: (0, 0)), tile, vec, vec],
        out_specs=[tile, tile],
        out_shape=[jax.ShapeDtypeStruct((m, d), F32), jax.ShapeDtypeStruct((m, d), BF16)],
        compiler_params=_params("parallel"),
        name="out_ln",
    )(merged, w_out, h, ln_g.astype(F32).reshape(1, d), ln_b.astype(F32).reshape(1, d))


def kernel(x, w_in, rel_bias, da_lambda, da_subln, hg_lower, hg_norm, rw_mu, rw_w0, rw_w2, rw_a0, rw_a2, rw_v1,
           rw_v_mu, rw_v0, rw_v2, rw_kk, rw_ka, rw_rk, rw_lnx_g, rw_lnx_b, w_branch, w_out, ln_g, ln_b):
    b, s, d = x.shape
    depth = w_in.shape[0]
    width = w_branch.shape[2]
    heads = width // V7X_LANES
    m = b * s
    alpha = (2 * depth) ** 0.25
    gb = width // V7X_LANES

    off_a, off_b, off_c = 0, 4 * width, 8 * width
    rw_mix = 3 * width + RW_W_RANK + RW_A_RANK
    off_cg = off_c + rw_mix
    off_d = off_cg + width
    off_m = off_d + 4 * width
    col_c, col_cg, col_a, col_b, col_d = 0, 3 * gb, 4 * gb, 8 * gb, 12 * gb
    col_m = 16 * gb

    lbs = jnp.cumsum(jax.nn.softmax(hg_lower.astype(F32), axis=0), axis=0)
    lbs = lbs - lbs[0:1]
    bias_tiles = _bias_tiles(rel_bias, min(ATT_TILE, s))

    h = x.reshape(m, d)
    hb = h.astype(BF16)
    v_first = None
    for l in range(depth):
        wl = w_in[l]
        w_main = jnp.concatenate([wl[:, off_c:off_c + 3 * width], wl[:, off_cg:off_cg + width],
                                  wl[:, off_a:off_a + 4 * width], wl[:, off_b:off_b + 4 * width],
                                  wl[:, off_d:off_d + 4 * width], wl[:, off_m:]], axis=1).astype(BF16)
        low_cols = [wl[:, off_c + 3 * width:off_cg]]
        if l > 0:
            low_cols.append(rw_v1[l - 1])
        n_low = sum(c.shape[1] for c in low_cols)
        low_cols.append(jnp.zeros((d, RW_LOW - n_low), F32))
        w_low = jnp.concatenate(low_cols, axis=1).astype(BF16)

        z2d = _matmul(hb, w_main, BF16, min(1024, m), 2048, "in_proj")
        low = _matmul(hb, w_low, F32, min(1024, m), RW_LOW, "in_proj_low").reshape(b, s, RW_LOW)
        z = z2d.reshape(b, s, -1)

        y_a = _diff_attention(z, (col_a, col_a + gb, col_a + 2 * gb, col_a + 3 * gb), bias_tiles, da_lambda[l],
                              da_subln[l], l, heads)
        y_b = _hgrn2(z, (col_b, col_b + gb, col_b + 2 * gb, col_b + 3 * gb), lbs[l], hg_norm[l], heads)
        y_c, v_first = _rwkv7(z, z, low, v_first, rw_mu[l], rw_w0[l], rw_w2[l], rw_a0[l], rw_a2[l],
                              rw_v_mu[l - 1] if l > 0 else None, rw_v0[l - 1] if l > 0 else None,
                              rw_v2[l - 1] if l > 0 else None, rw_kk[l], rw_ka[l], rw_rk[l], rw_lnx_g[l],
                              rw_lnx_b[l], zm_col=col_c, zg_col=col_cg)
        y_d = _stick_breaking(z, (col_d, col_d + gb, col_d + 2 * gb, col_d + 3 * gb), heads, V7X_LANES)

        merged = _merge([y.reshape(m, width) for y in (y_a, y_b, y_c, y_d)], z2d, col_m * V7X_LANES // d,
                        w_branch[l].astype(BF16))
        h, hb = _out_ln(merged, w_out[l].astype(BF16), h, ln_g[l], ln_b[l], alpha)
    return h.reshape(b, s, d)
```
